```python
import math
import jax
import jax.numpy as jnp
from jax import lax
import numpy as np

D_MODEL = 1024
BATCH = 1
SEQ = 16384
DEPTH = 2
DEC_BATCH = 16
DEC_SEQ = 2048
PAST_LEN = 128

HEAD_DIM = 64
DN_HEADS = 4
DN_WIDTH = DN_HEADS * HEAD_DIM
DN_CHUNK = 64
CONV_WIDTH = 5
DIL_HEADS = 4
DIL_WIDTH = DIL_HEADS * HEAD_DIM
DIL_PATTERNS = ((128, 1), (512, 4), (2048, 16))
SSM_WIDTH = 256
SSM_GROUP = 16
SSM_GROUPS = SSM_WIDTH // SSM_GROUP
SSM_STATE = 64
NA_HEADS = 4
NA_WIDTH = NA_HEADS * HEAD_DIM
GRID_W = 64
NA_WIN_R = 8
NA_WIN_C = 16

MIX_WIDTH = DN_WIDTH + DIL_WIDTH + SSM_WIDTH + NA_WIDTH
IN_SIZES = (3 * DN_WIDTH, DN_WIDTH, 2 * DN_HEADS, 2 * DN_HEADS, 3 * DIL_WIDTH, SSM_WIDTH, 3 * NA_WIDTH)
IN_COLS = 3 * DN_WIDTH + DN_WIDTH + 4 * DN_HEADS + 3 * DIL_WIDTH + SSM_WIDTH + 3 * NA_WIDTH
D_FF = 4 * D_MODEL
ROPE_THETA = 10000.0
EPS = 1e-6
NEG = -1e30

kernel_name = 'hybrid_parallel_head_encoder'


def rms_norm(x, w):
    xf = x.astype(jnp.float32)
    y = xf * lax.rsqrt(jnp.mean(xf * xf, axis=-1, keepdims=True) + EPS)
    return (y * w.astype(jnp.float32)).astype(x.dtype)


def l2_normalise(t):
    return t * lax.rsqrt(jnp.sum(t * t, axis=-1, keepdims=True) + EPS)


def rotary_tables(seq):
    inv_freq = ROPE_THETA ** (-jnp.arange(0, HEAD_DIM, 2, dtype=jnp.float32) / HEAD_DIM)
    ang = jnp.arange(seq, dtype=jnp.float32)[:, None] * inv_freq[None, :]
    return jnp.cos(ang), jnp.sin(ang)


def rotary(t, cos, sin):
    t1, t2 = jnp.split(t, 2, axis=-1)
    c, s = cos[None, :, None, :], sin[None, :, None, :]
    return jnp.concatenate([t1 * c - t2 * s, t2 * c + t1 * s], axis=-1)


def split_heads3(t, heads):
    bsz, seq, _ = t.shape
    q, k, v = jnp.split(t, 3, axis=-1)
    return (q.reshape(bsz, seq, heads, HEAD_DIM), k.reshape(bsz, seq, heads, HEAD_DIM),
            v.reshape(bsz, seq, heads, HEAD_DIM))


def centred_depthwise_conv(x, w):
    ch = x.shape[-1]
    half = w.shape[0] // 2
    return lax.conv_general_dilated(x, w[:, None, :], window_strides=(1,), padding=[(half, half)],
                                    dimension_numbers=('NWC', 'WIO', 'NWC'), feature_group_count=ch)


def delta_rule_chunked(q, k, v, g, beta):
    bsz, heads, seq, hd = q.shape
    n = seq // DN_CHUNK
    q, k, v = (t.reshape(bsz, heads, n, DN_CHUNK, hd) for t in (q, k, v))
    g = g.reshape(bsz, heads, n, DN_CHUNK)
    beta = beta.reshape(bsz, heads, n, DN_CHUNK)
    gam = jnp.cumsum(g, axis=-1)
    incl = jnp.tril(jnp.ones((DN_CHUNK, DN_CHUNK), dtype=bool))
    strict = jnp.tril(jnp.ones((DN_CHUNK, DN_CHUNK), dtype=bool), -1)
    decay = jnp.exp(jnp.where(incl, gam[..., :, None] - gam[..., None, :], -jnp.inf))
    kb = k * beta[..., None]
    lower = jnp.where(strict, jnp.einsum('bhncd,bhnkd->bhnck', kb, k) * decay, 0.0)
    eye = jnp.eye(DN_CHUNK, dtype=q.dtype)
    rhs = jnp.concatenate([v * beta[..., None], kb * jnp.exp(gam)[..., None]], axis=-1)
    sol = lax.linalg.triangular_solve(eye + lower, rhs, left_side=True, lower=True, unit_diagonal=True)
    u, w = jnp.split(sol, 2, axis=-1)
    attn = jnp.einsum('bhncd,bhnkd->bhnck', q, k) * decay
    qg = q * jnp.exp(gam)[..., None]
    kd = k * jnp.exp(gam[..., -1:] - gam)[..., None]
    gl = jnp.exp(gam[..., -1])

    def step(state, xs):
        qg_i, kd_i, u_i, w_i, attn_i, gl_i = xs
        v_new = u_i - jnp.einsum('bhcd,bhde->bhce', w_i, state)
        o = jnp.einsum('bhcd,bhde->bhce', qg_i, state) + jnp.einsum('bhck,bhke->bhce', attn_i, v_new)
        state = state * gl_i[..., None, None] + jnp.einsum('bhcd,bhce->bhde', kd_i, v_new)
        return state, o

    xs = tuple(jnp.moveaxis(t, 2, 0) for t in (qg, kd, u, w, attn, gl))
    state0 = jnp.zeros((bsz, heads, hd, hd), q.dtype)
    _, o = lax.scan(step, state0, xs)
    return jnp.moveaxis(o, 0, 2).reshape(bsz, heads, seq, hd)


def deltanet_mixer(qkv, gate, a_in, b_in, conv_w, a_log, dt_bias, norm_w):
    bsz, seq, _ = qkv.shape
    qkv = jax.nn.silu(centred_depthwise_conv(qkv, conv_w.astype(jnp.float32)))
    q, k, v = split_heads3(qkv, DN_HEADS)
    q, k, v = (t.transpose(0, 2, 1, 3) for t in (q, k, v))
    q = l2_normalise(q) * HEAD_DIM ** -0.5
    k = l2_normalise(k)
    a_in = a_in.reshape(bsz, seq, 2, DN_HEADS)
    b_in = b_in.reshape(bsz, seq, 2, DN_HEADS)
    g = -jnp.exp(a_log.astype(jnp.float32)) * jax.nn.softplus(a_in + dt_bias.astype(jnp.float32))
    beta = jax.nn.sigmoid(b_in)
    g = g.transpose(2, 0, 3, 1)
    beta = beta.transpose(2, 0, 3, 1)
    flip = lambda t: jnp.flip(t, axis=2)
    o_f = delta_rule_chunked(q, k, v, g[0], beta[0])
    o_b = flip(delta_rule_chunked(flip(q), flip(k), flip(v), flip(g[1]), flip(beta[1])))
    o = (o_f + o_b).transpose(0, 2, 1, 3)
    o = rms_norm(o, norm_w) * jax.nn.silu(gate.reshape(bsz, seq, DN_HEADS, HEAD_DIM))
    return o.reshape(bsz, seq, DN_WIDTH)


def dilated_branch(q, k, v, dil, half):
    bsz, seq, heads, hd = q.shape
    length = seq // dil
    nb = -(-length // half)
    lp = nb * half

    def by_residue(t, lo, hi):
        t = t.reshape(bsz, length, dil, heads, hd).transpose(0, 2, 1, 3, 4)
        return jnp.pad(t, ((0, 0), (0, 0), (lo, hi), (0, 0), (0, 0)))

    qr = by_residue(q, 0, lp - length).reshape(bsz, dil, nb, half, heads, hd)
    kr = by_residue(k, half, lp - length + half).reshape(bsz, dil, nb + 2, half, heads, hd)
    vr = by_residue(v, half, lp - length + half).reshape(bsz, dil, nb + 2, half, heads, hd)
    band = lambda t: jnp.concatenate([t[:, :, :-2], t[:, :, 1:-1], t[:, :, 2:]], axis=3)
    kb, vb = band(kr), band(vr)
    mq = jnp.arange(nb)[:, None] * half + jnp.arange(half)[None, :]
    mk = jnp.arange(nb)[:, None] * half - half + jnp.arange(3 * half)[None, :]
    ok = ((jnp.abs(mk[:, None, :] - mq[:, :, None]) <= half)
          & (mk[:, None, :] >= 0) & (mk[:, None, :] < length))
    s = jnp.einsum('bgnqhd,bgnkhd->bgnhqk', qr, kb) * hd ** -0.5
    s = jnp.where(ok[None, None, :, None], s, NEG)
    m = jnp.max(s, axis=-1)
    p = jnp.exp(s - m[..., None])
    l = jnp.sum(p, axis=-1)
    o = jnp.einsum('bgnhqk,bgnkhd->bgnqhd', p, vb) / jnp.swapaxes(l, -1, -2)[..., None]
    o = o.reshape(bsz, dil, lp, heads, hd)[:, :, :length].transpose(0, 2, 1, 3, 4).reshape(bsz, seq, heads, hd)
    back = lambda t: jnp.swapaxes(t, -1, -2).reshape(bsz, dil, lp, heads)[:, :, :length].transpose(0, 2, 1, 3).reshape(bsz, seq, heads)
    return o, back(m), back(l)


def dilated_attention(q, k, v):
    outs, maxes, dens = [], [], []
    for window, dil in DIL_PATTERNS:
        o, m, l = dilated_branch(q, k, v, dil, window // (2 * dil))
        outs.append(o)
        maxes.append(m)
        dens.append(l)
    o, m, l = jnp.stack(outs), jnp.stack(maxes), jnp.stack(dens)
    wgt = l * jnp.exp(m - jnp.max(m, axis=0, keepdims=True))
    return jnp.sum(wgt[..., None] * o, axis=0) / jnp.sum(wgt, axis=0)[..., None]


def complex_affine_combine(e1, e2):
    a1r, a1i, b1r, b1i = e1
    a2r, a2i, b2r, b2i = e2
    return (a2r * a1r - a2i * a1i, a2r * a1i + a2i * a1r,
            a2r * b1r - a2i * b1i + b2r, a2r * b1i + a2i * b1r + b2i)


def s5_direction(u, lam_re, lam_im, log_dt, b_re, b_im, c_re, c_im, reverse):
    seq = u.shape[1]
    dt = jnp.exp(log_dt)[:, None]
    mag = jnp.exp(lam_re * dt)
    ar, ai = mag * jnp.cos(lam_im * dt), mag * jnp.sin(lam_im * dt)
    den = lam_re * lam_re + lam_im * lam_im
    fr = ((ar - 1.0) * lam_re + ai * lam_im) / den
    fi = (ai * lam_re - (ar - 1.0) * lam_im) / den
    bbr = fr[..., None] * b_re - fi[..., None] * b_im
    bbi = fr[..., None] * b_im + fi[..., None] * b_re
    xr = jnp.einsum('bsgp,gnp->bsgn', u, bbr)
    xi = jnp.einsum('bsgp,gnp->bsgn', u, bbi)
    shape = (1, seq) + ar.shape
    elems = (jnp.broadcast_to(ar, shape), jnp.broadcast_to(ai, shape), xr, xi)
    _, _, hr, hi = lax.associative_scan(complex_affine_combine, elems, reverse=reverse, axis=1)
    return jnp.einsum('bsgn,gpn->bsgp', hr, c_re) - jnp.einsum('bsgn,gpn->bsgp', hi, c_im)


def s5_mixer(u, lam_re, lam_im, log_dt, b_re, b_im, c_re, c_im, d_skip, glu_w, glu_b):
    bsz, seq, _ = u.shape
    f = lambda t: t.astype(jnp.float32)
    ug = u.reshape(bsz, seq, SSM_GROUPS, SSM_GROUP)
    y = (s5_direction(ug, f(lam_re[0]), f(lam_im[0]), f(log_dt[0]), f(b_re[0]), f(b_im[0]), f(c_re[0]), f(c_im[0]), False)
         + s5_direction(ug, f(lam_re[1]), f(lam_im[1]), f(log_dt[1]), f(b_re[1]), f(b_im[1]), f(c_re[1]), f(c_im[1]), True))
    y = y.reshape(bsz, seq, SSM_WIDTH) + f(d_skip) * u
    z = jax.nn.gelu(y)
    return z * jax.nn.sigmoid(z @ f(glu_w) + f(glu_b))


def neighbourhood_attention(q, k, v, rpb):
    bsz, seq, heads, hd = q.shape
    rows = seq // GRID_W
    wr = min(NA_WIN_R, rows)
    grid = lambda t: t.reshape(bsz, rows, GRID_W, heads, hd)
    qg, kg, vg = grid(q), grid(k), grid(v)
    r = jnp.arange(rows)
    ridx = jnp.clip(r - wr // 2, 0, rows - wr)[:, None] + jnp.arange(wr)[None, :]
    c = jnp.arange(GRID_W)
    cstart = jnp.clip(c - NA_WIN_C // 2, 0, GRID_W - NA_WIN_C)
    col_ok = (c[None, :] >= cstart[:, None]) & (c[None, :] < cstart[:, None] + NA_WIN_C)
    kk, vv = kg[:, ridx], vg[:, ridx]
    s = jnp.einsum('brqhd,brwkhd->brhqwk', qg, kk) * hd ** -0.5
    dr = ridx - r[:, None] + (NA_WIN_R - 1)
    dc = jnp.clip(c[None, :] - c[:, None] + (NA_WIN_C - 1), 0, 2 * NA_WIN_C - 2)
    bias = rpb.astype(jnp.float32)[:, dr[:, None, :, None], dc[None, :, None, :]]
    s = s + jnp.transpose(bias, (1, 0, 2, 3, 4))[None]
    s = jnp.where(col_ok[:, None, :], s, NEG)
    p = jax.nn.softmax(s.reshape(bsz, rows, heads, GRID_W, wr * GRID_W), axis=-1).reshape(s.shape)
    o = jnp.einsum('brhqwk,brwkhd->brqhd', p, vv)
    return o.reshape(bsz, seq, heads, hd)


def encoder_trunk(x, ln1_w, w_in, dn_conv_w, dn_a_log, dn_dt_bias, dn_norm_w,
                  ssm_lam_re, ssm_lam_im, ssm_log_dt, ssm_b_re, ssm_b_im, ssm_c_re, ssm_c_im,
                  ssm_d, ssm_glu_w, ssm_glu_b, na_rpb, w_out, ln2_w, w_ff1, w_ff2, final_norm_w):
    bsz, seq, _ = x.shape
    cos, sin = rotary_tables(seq)
    splits = [int(i) for i in np.cumsum(IN_SIZES)[:-1]]
    for i in range(DEPTH):
        h = rms_norm(x, ln1_w[i])
        z = (h @ w_in[i]).astype(jnp.float32)
        dn_qkv, dn_gate, dn_a, dn_b, dil_qkv, ssm_u, na_qkv = jnp.split(z, splits, axis=-1)
        y_a = deltanet_mixer(dn_qkv, dn_gate, dn_a, dn_b, dn_conv_w[i], dn_a_log[i], dn_dt_bias[i], dn_norm_w[i])
        q, k, v = split_heads3(dil_qkv, DIL_HEADS)
        y_b = dilated_attention(rotary(q, cos, sin), rotary(k, cos, sin), v).reshape(bsz, seq, DIL_WIDTH)
        y_c = s5_mixer(ssm_u, ssm_lam_re[i], ssm_lam_im[i], ssm_log_dt[i], ssm_b_re[i], ssm_b_im[i],
                       ssm_c_re[i], ssm_c_im[i], ssm_d[i], ssm_glu_w[i], ssm_glu_b[i])
        q, k, v = split_heads3(na_qkv, NA_HEADS)
        y_d = neighbourhood_attention(q, k, v, na_rpb[i]).reshape(bsz, seq, NA_WIDTH)
        mix = jnp.concatenate([y_a, y_b, y_c, y_d], axis=-1).astype(x.dtype)
        x = x + mix @ w_out[i]
        h = rms_norm(x, ln2_w[i])
        x = x + jnp.square(jax.nn.relu(h @ w_ff1[i])) @ w_ff2[i]
    return rms_norm(x, final_norm_w)


def setup_inputs(seed: int = 0) -> dict:
    key = jax.random.key(seed)
    ks = jax.random.split(key, 26)
    f32 = jnp.float32
    nrm = lambda k, shape, scale: scale * jax.random.normal(k, shape, f32)
    x_prompt = nrm(ks[0], (BATCH, SEQ, D_MODEL), 1.0)
    x_sample = nrm(ks[1], (DEC_BATCH, DEC_SEQ, D_MODEL), 1.0)
    ln1_w = 1.0 + nrm(ks[2], (DEPTH, D_MODEL), 0.02)
    w_in = nrm(ks[3], (DEPTH, D_MODEL, IN_COLS), D_MODEL ** -0.5)
    dn_conv_w = nrm(ks[4], (DEPTH, CONV_WIDTH, 3 * DN_WIDTH), CONV_WIDTH ** -0.5)
    dn_a_log = jnp.log(jax.random.uniform(ks[5], (DEPTH, 2, DN_HEADS), f32, 1.0, 16.0))
    dt = jnp.exp(jax.random.uniform(ks[6], (DEPTH, 2, DN_HEADS), f32, math.log(1e-3), math.log(1e-1)))
    dn_dt_bias = dt + jnp.log(-jnp.expm1(-dt))
    dn_norm_w = 1.0 + nrm(ks[7], (DEPTH, HEAD_DIM), 0.02)
    lam_shape = (DEPTH, 2, SSM_GROUPS, SSM_STATE)
    ssm_lam_re = -0.5 + nrm(ks[8], lam_shape, 0.01)
    ssm_lam_im = math.pi * jnp.arange(SSM_STATE, dtype=f32) + nrm(ks[9], lam_shape, 0.01)
    ssm_log_dt = jax.random.uniform(ks[10], (DEPTH, 2, SSM_GROUPS), f32, math.log(1e-3), math.log(1e-1))
    b_shape = (DEPTH, 2, SSM_GROUPS, SSM_STATE, SSM_GROUP)
    ssm_b_re = nrm(ks[11], b_shape, (2 * SSM_GROUP) ** -0.5)
    ssm_b_im = nrm(ks[12], b_shape, (2 * SSM_GROUP) ** -0.5)
    c_shape = (DEPTH, 2, SSM_GROUPS, SSM_GROUP, SSM_STATE)
    ssm_c_re = nrm(ks[13], c_shape, SSM_STATE ** -0.5)
    ssm_c_im = nrm(ks[14], c_shape, SSM_STATE ** -0.5)
    ssm_d = nrm(ks[15], (DEPTH, SSM_WIDTH), 1.0)
    ssm_glu_w = nrm(ks[16], (DEPTH, SSM_WIDTH, SSM_WIDTH), SSM_WIDTH ** -0.5)
    ssm_glu_b = nrm(ks[17], (DEPTH, SSM_WIDTH), 0.02)
    na_rpb = nrm(ks[18], (DEPTH, NA_HEADS, 2 * NA_WIN_R - 1, 2 * NA_WIN_C - 1), 0.1)
    w_out = nrm(ks[19], (DEPTH, MIX_WIDTH, D_MODEL), MIX_WIDTH ** -0.5)
    ln2_w = 1.0 + nrm(ks[20], (DEPTH, D_MODEL), 0.02)
    w_ff1 = nrm(ks[21], (DEPTH, D_MODEL, D_FF), D_MODEL ** -0.5)
    w_ff2 = nrm(ks[22], (DEPTH, D_FF, D_MODEL), D_FF ** -0.5)
    final_norm_w = 1.0 + nrm(ks[23], (D_MODEL,), 0.02)
    return {'x_prompt': x_prompt, 'x_sample': x_sample, 'ln1_w': ln1_w, 'w_in': w_in,
            'dn_conv_w': dn_conv_w, 'dn_a_log': dn_a_log, 'dn_dt_bias': dn_dt_bias, 'dn_norm_w': dn_norm_w,
            'ssm_lam_re': ssm_lam_re, 'ssm_lam_im': ssm_lam_im, 'ssm_log_dt': ssm_log_dt,
            'ssm_b_re': ssm_b_re, 'ssm_b_im': ssm_b_im, 'ssm_c_re': ssm_c_re, 'ssm_c_im': ssm_c_im,
            'ssm_d': ssm_d, 'ssm_glu_w': ssm_glu_w, 'ssm_glu_b': ssm_glu_b, 'na_rpb': na_rpb,
            'w_out': w_out, 'ln2_w': ln2_w, 'w_ff1': w_ff1, 'w_ff2': w_ff2, 'final_norm_w': final_norm_w}


def reference(x_prompt, x_sample, ln1_w, w_in, dn_conv_w, dn_a_log, dn_dt_bias, dn_norm_w,
              ssm_lam_re, ssm_lam_im, ssm_log_dt, ssm_b_re, ssm_b_im, ssm_c_re, ssm_c_im,
              ssm_d, ssm_glu_w, ssm_glu_b, na_rpb, w_out, ln2_w, w_ff1, w_ff2, final_norm_w):
    weights = dict(ln1_w=ln1_w, w_in=w_in, dn_conv_w=dn_conv_w, dn_a_log=dn_a_log, dn_dt_bias=dn_dt_bias,
                   dn_norm_w=dn_norm_w, ssm_lam_re=ssm_lam_re, ssm_lam_im=ssm_lam_im, ssm_log_dt=ssm_log_dt,
                   ssm_b_re=ssm_b_re, ssm_b_im=ssm_b_im, ssm_c_re=ssm_c_re, ssm_c_im=ssm_c_im, ssm_d=ssm_d,
                   ssm_glu_w=ssm_glu_w, ssm_glu_b=ssm_glu_b, na_rpb=na_rpb, w_out=w_out, ln2_w=ln2_w,
                   w_ff1=w_ff1, w_ff2=w_ff2, final_norm_w=final_norm_w)
    y_prompt = encoder_trunk(x_prompt, **weights)
    y_sample = encoder_trunk(x_sample, **weights)
    return (y_prompt, y_sample)
```

```python
import functools
import math

import numpy as np
import jax
import jax.numpy as jnp
from jax import lax
from jax.experimental import pallas as pl
from jax.experimental.pallas import tpu as pltpu

F32 = jnp.float32
BF16 = jnp.bfloat16

D_MODEL = 1024
DEPTH = 2
HEAD_DIM = 64
HEADS = 4
WIDTH = HEADS * HEAD_DIM
DN_CHUNK = 64
CONV_WIDTH = 5
DIL_PATTERNS = ((128, 1), (512, 4), (2048, 16))
SSM_GROUP = 16
SSM_GROUPS = 16
SSM_STATE = 64
SSM_CHUNK = 64
GRID_W = 64
NA_WIN_R = 8
NA_WIN_C = 16
D_FF = 4 * D_MODEL
ROPE_THETA = 10000.0
EPS = 1e-6
NEG = -1e30

LANES = 128
TOKEN_TILE = 512
DIL_TILE = 512
DIL_REACH = 2
NA_ROWS_PER_STEP = 8
NA_HALO_ROWS = 4
IN_COLS_PADDED = 23 * LANES
VMEM_LIMIT = 56 * 2**20


def _params(sem):
    return pltpu.CompilerParams(dimension_semantics=sem, vmem_limit_bytes=VMEM_LIMIT)


def _dot(a, b):
    return jnp.dot(a, b, preferred_element_type=F32)


def _dot_nt(a, b):
    return lax.dot_general(a, b, (((1,), (1,)), ((), ())), preferred_element_type=F32)


def _bmm(a, b):
    return lax.dot_general(a, b, (((2,), (1,)), ((0,), (0,))), preferred_element_type=F32)


def _bmm_nt(a, b):
    return lax.dot_general(a, b, (((2,), (2,)), ((0,), (0,))), preferred_element_type=F32)


def _split2(x):
    hi = x.astype(BF16)
    lo = (x - hi.astype(F32)).astype(BF16)
    return hi, lo


def _split3(x):
    hi = x.astype(BF16)
    r = x - hi.astype(F32)
    mid = r.astype(BF16)
    lo = (r - mid.astype(F32)).astype(BF16)
    return hi, mid, lo


def _exact_lhs_dot(m, x):
    hi, mid, lo = _split3(x)
    return _dot(m, hi) + _dot(m, mid) + _dot(m, lo)


def _head_sums(sq, bd):
    hi, lo = _split2(sq)
    return _dot(hi, bd) + _dot(lo, bd)


def _sigmoid(x):
    return 1.0 / (1.0 + jnp.exp(-x))


def _rms(x, w):
    ms = jnp.mean(x * x, axis=-1, keepdims=True)
    return x * lax.rsqrt(ms + EPS) * w


def _in_proj_kernel(x_ref, lnw_ref, w_ref, cos_ref, sin_ref,
                    dn_ref, gate_ref, ab_ref, dq_ref, dk_ref, dv_ref, ssm_ref, nq_ref, nk_ref, nv_ref):
    h = _rms(x_ref[...], lnw_ref[...]).astype(BF16)
    z = _dot(h, w_ref[...])
    tm = z.shape[0]
    W = WIDTH
    dn_ref[...] = z[:, 0:3 * W]
    gate_ref[...] = z[:, 3 * W:4 * W]
    cos = cos_ref[...]
    sin = sin_ref[...]
    lane = lax.broadcasted_iota(jnp.int32, (tm, W), 1)
    first_half = (lane % HEAD_DIM) < (HEAD_DIM // 2)

    def rope(t):
        rot = jnp.where(first_half, pltpu.roll(t, W - HEAD_DIM // 2, 1), pltpu.roll(t, HEAD_DIM // 2, 1))
        return t * cos + rot * sin

    scale = HEAD_DIM ** -0.5
    dq_ref[...] = (rope(z[:, 4 * W:5 * W]) * scale).astype(BF16)
    dk_ref[...] = rope(z[:, 5 * W:6 * W]).astype(BF16)
    dv_ref[...] = z[:, 6 * W:7 * W].astype(BF16)
    ssm_ref[...] = z[:, 7 * W:8 * W]
    nq_ref[...] = (z[:, 8 * W:9 * W] * scale).astype(BF16)
    nk_ref[...] = z[:, 9 * W:10 * W].astype(BF16)
    nv_ref[...] = z[:, 10 * W:11 * W].astype(BF16)
    ab_ref[...] = z[:, 11 * W:11 * W + LANES]


def _in_proj(x, lnw, w, cos_t, sin_t, seq):
    T = x.shape[0]
    tm = min(TOKEN_TILE, seq)
    per_seq = seq // tm
    W = WIDTH
    row = lambda n: pl.BlockSpec((tm, n), lambda i: (i, 0))
    const = lambda a: pl.BlockSpec(a.shape, lambda i: (0, 0))
    tab = pl.BlockSpec((tm, W), lambda i: (i % per_seq, 0))
    out_shapes = [
        jax.ShapeDtypeStruct((T, 3 * W), F32), jax.ShapeDtypeStruct((T, W), F32),
        jax.ShapeDtypeStruct((T, LANES), F32),
        jax.ShapeDtypeStruct((T, W), BF16), jax.ShapeDtypeStruct((T, W), BF16), jax.ShapeDtypeStruct((T, W), BF16),
        jax.ShapeDtypeStruct((T, W), F32),
        jax.ShapeDtypeStruct((T, W), BF16), jax.ShapeDtypeStruct((T, W), BF16), jax.ShapeDtypeStruct((T, W), BF16),
    ]
    out_specs = [row(3 * W), row(W), row(LANES), row(W), row(W), row(W), row(W), row(W), row(W), row(W)]
    return pl.pallas_call(
        _in_proj_kernel, grid=(T // tm,),
        in_specs=[row(D_MODEL), const(lnw), const(w), tab, tab],
        out_specs=out_specs, out_shape=out_shapes,
        compiler_params=_params(("parallel",)), name="in_proj",
    )(x, lnw, w, cos_t, sin_t)


def _dn_prep_kernel(zc_ref, zp_ref, zn_ref, ab_ref, cw_ref, alog_ref, dtb_ref, bd_ref,
                    q_ref, k_ref, v_ref, gb_ref, buf):
    i = pl.program_id(1)
    n = pl.num_programs(1)
    tb = zc_ref.shape[1]
    W = WIDTH
    half = CONV_WIDTH // 2
    buf[0:8, :] = jnp.where(i > 0, zp_ref[0], 0.0)
    buf[8:8 + tb, :] = zc_ref[0]
    buf[8 + tb:16 + tb, :] = jnp.where(i < n - 1, zn_ref[0], 0.0)
    cw = cw_ref[...]
    acc = buf[8 - half:8 - half + tb, :] * cw[0:1, :]
    for j in range(1, CONV_WIDTH):
        acc = acc + buf[8 - half + j:8 - half + j + tb, :] * cw[j:j + 1, :]
    y = acc * _sigmoid(acc)
    bd = bd_ref[...]
    q = y[:, 0:W]
    k = y[:, W:2 * W]
    q_ref[0] = q * lax.rsqrt(_head_sums(q * q, bd) + EPS) * (HEAD_DIM ** -0.5)
    k_ref[0] = k * lax.rsqrt(_head_sums(k * k, bd) + EPS)
    v_ref[0] = y[:, 2 * W:3 * W]
    ab = ab_ref[0]
    t = ab + dtb_ref[...]
    softplus = jnp.maximum(t, 0.0) + jnp.log(1.0 + jnp.exp(-jnp.abs(t)))
    g = -jnp.exp(alog_ref[...]) * softplus
    lane = lax.broadcasted_iota(jnp.int32, ab.shape, 1)
    gb_ref[0] = jnp.where(lane < 2 * HEADS, g, _sigmoid(ab))


def _dn_prep(z_dn, z_ab, conv_w, alog, dtb, bd):
    B, S, C = z_dn.shape
    tb = min(TOKEN_TILE, S)
    nb = S // tb
    r8 = tb // 8
    W = WIDTH
    blk = lambda n: pl.BlockSpec((1, tb, n), lambda b, i: (b, i, 0))
    const = lambda a: pl.BlockSpec(a.shape, lambda b, i: (0, 0))
    prev = pl.BlockSpec((1, 8, C), lambda b, i: (b, jnp.maximum(i * r8 - 1, 0), 0))
    nxt = pl.BlockSpec((1, 8, C), lambda b, i: (b, jnp.minimum((i + 1) * r8, S // 8 - 1), 0))
    return pl.pallas_call(
        _dn_prep_kernel, grid=(B, nb),
        in_specs=[blk(C), prev, nxt, blk(LANES), const(conv_w), const(alog), const(dtb), const(bd)],
        out_specs=[blk(W), blk(W), blk(W), blk(LANES)],
        out_shape=[jax.ShapeDtypeStruct((B, S, W), F32)] * 3 + [jax.ShapeDtypeStruct((B, S, LANES), F32)],
        scratch_shapes=[pltpu.VMEM((tb + 16, C), F32)],
        compiler_params=_params(("parallel", "parallel")), name="dn_prep",
    )(z_dn, z_dn, z_dn, z_ab, conv_w, alog, dtb, bd)


def _dn_scan_kernel(q_ref, k_ref, v_ref, gb_ref, o_ref,
                    s_scr, u_scr, w_scr, qg_scr, at_scr, kdt_scr, *, rev, nc):
    C = DN_CHUNK
    tb = nc * C
    H = HEADS

    @pl.when(pl.program_id(1) == 0)
    def _():
        s_scr[...] = jnp.zeros_like(s_scr)

    gb = gb_ref[0]
    r = lax.broadcasted_iota(jnp.int32, (tb, tb), 0)
    c = lax.broadcasted_iota(jnp.int32, (tb, tb), 1)
    same = (r // C) == (c // C)
    cum = same & ((c >= r) if rev else (c <= r))
    gam = _exact_lhs_dot(jnp.where(cum, 1.0, 0.0).astype(BF16), gb)
    gtot = _exact_lhs_dot(jnp.where(same, 1.0, 0.0).astype(BF16), gb)
    ri = lax.broadcasted_iota(jnp.int32, (LANES, LANES), 0)
    ci = lax.broadcasted_iota(jnp.int32, (LANES, LANES), 1)
    eye128 = jnp.where(ri == ci, 1.0, 0.0).astype(BF16)
    g_hi, g_mid, g_lo = _split3(gam)
    gam_t = _dot_nt(eye128, g_hi) + _dot_nt(eye128, g_mid) + _dot_nt(eye128, g_lo)
    eye64 = eye128[0:C, 0:C]

    ii = lax.broadcasted_iota(jnp.int32, (tb, C), 0) % C
    jj = lax.broadcasted_iota(jnp.int32, (tb, C), 1)
    incl = (jj >= ii) if rev else (jj <= ii)
    strict = (jj > ii) if rev else (jj < ii)
    eye_f = jnp.where(ii == jj, 1.0, 0.0)

    for h in range(H):
        ch = (H if rev else 0) + h
        gam_c = gam[:, ch:ch + 1]
        gtot_c = gtot[:, ch:ch + 1]
        beta = gb[:, 2 * H + ch:2 * H + ch + 1]
        grow = gam_t[ch:ch + 1, :]
        g_rows = jnp.concatenate(
            [jnp.broadcast_to(grow[:, cc * C:(cc + 1) * C], (C, C)) for cc in range(nc)], axis=0)
        dec = jnp.exp(jnp.where(incl, gam_c - g_rows, NEG))
        qh = q_ref[0, :, h * C:(h + 1) * C]
        kh = k_ref[0, :, h * C:(h + 1) * C]
        vh = v_ref[0, :, h * C:(h + 1) * C]
        kb = kh * beta
        k3 = kh.astype(BF16).reshape(nc, C, C)
        a_raw = _bmm_nt(kb.astype(BF16).reshape(nc, C, C), k3).reshape(tb, C)
        qk = _bmm_nt(qh.astype(BF16).reshape(nc, C, C), k3).reshape(tb, C)
        lm = jnp.where(strict, a_raw * dec, 0.0)
        attn = qk * dec
        t_inv = eye_f - jnp.where((ii // 2) == (jj // 2), lm, 0.0)
        n = 2
        while n < C:
            off = ((ii // (2 * n)) == (jj // (2 * n))) & ((ii // n) != (jj // n))
            bn = jnp.where(off, lm, 0.0).astype(BF16).reshape(nc, C, C)
            t3 = t_inv.astype(BF16).reshape(nc, C, C)
            bt = _bmm(bn, t3)
            t_inv = t_inv - _bmm(t3, bt.astype(BF16)).reshape(tb, C)
            n *= 2
        eg = jnp.exp(gam_c)
        rhs = jnp.concatenate([vh * beta, kb * eg], axis=1)
        t_hi, t_lo = _split2(t_inv)
        r_hi, r_lo = _split2(rhs)
        t_hi3 = t_hi.reshape(nc, C, C)
        sol = (_bmm(t_hi3, r_hi.reshape(nc, C, 2 * C)) + _bmm(t_hi3, r_lo.reshape(nc, C, 2 * C))
               + _bmm(t_lo.reshape(nc, C, C), r_hi.reshape(nc, C, 2 * C))).reshape(tb, 2 * C)
        u_scr[h] = sol[:, 0:C]
        w_scr[h] = sol[:, C:2 * C].astype(BF16)
        qg_scr[h] = (qh * eg).astype(BF16)
        at_scr[h] = attn.astype(BF16)
        kd = kh * jnp.exp(gtot_c - gam_c)
        kdt_scr[h] = _dot_nt(eye64, kd.astype(BF16)).astype(BF16)

    order = range(nc - 1, -1, -1) if rev else range(nc)
    for cc in order:
        lo, hi = cc * C, (cc + 1) * C
        for h in range(H):
            ch = (H if rev else 0) + h
            s = s_scr[h]
            sb = s.astype(BF16)
            v_new = u_scr[h, lo:hi, :] - _dot(w_scr[h, lo:hi, :], sb)
            vb = v_new.astype(BF16)
            o = _dot(qg_scr[h, lo:hi, :], sb) + _dot(at_scr[h, lo:hi, :], vb)
            o_ref[0, lo:hi, h * C:(h + 1) * C] = o
            gl = jnp.exp(gtot[lo:lo + 1, ch:ch + 1])
            s_scr[h] = s * gl + _dot(kdt_scr[h, :, lo:hi], vb)


def _dn_scan(q, k, v, gb, rev):
    B, S, W = q.shape
    tb = min(TOKEN_TILE, S)
    nb = S // tb
    nc = tb // DN_CHUNK
    C = DN_CHUNK
    if rev:
        imap = lambda b, i: (b, nb - 1 - i, 0)
    else:
        imap = lambda b, i: (b, i, 0)
    blk = lambda n: pl.BlockSpec((1, tb, n), imap)
    return pl.pallas_call(
        functools.partial(_dn_scan_kernel, rev=rev, nc=nc), grid=(B, nb),
        in_specs=[blk(W), blk(W), blk(W), blk(LANES)],
        out_specs=blk(W), out_shape=jax.ShapeDtypeStruct((B, S, W), F32),
        scratch_shapes=[pltpu.VMEM((HEADS, C, C), F32), pltpu.VMEM((HEADS, tb, C), F32),
                        pltpu.VMEM((HEADS, tb, C), BF16), pltpu.VMEM((HEADS, tb, C), BF16),
                        pltpu.VMEM((HEADS, tb, C), BF16), pltpu.VMEM((HEADS, C, tb), BF16)],
        compiler_params=_params(("parallel", "arbitrary")), name="dn_scan_bwd" if rev else "dn_scan_fwd",
    )(q, k, v, gb)


def _dil_bias_tiles():
    t = DIL_TILE
    qq = np.arange(t)[:, None]
    kk = np.arange(t)[None, :]
    tiles = []
    for o in range(-DIL_REACH, DIL_REACH + 1):
        delta = o * t + kk - qq
        cnt = np.zeros((t, t), np.float64)
        for window, dil in DIL_PATTERNS:
            cnt += ((delta % dil) == 0) & (np.abs(delta) <= window // 2)
        tiles.append(np.where(cnt > 0, np.log(np.maximum(cnt, 1.0)), NEG))
    return np.stack(tiles).astype(np.float32)


def _dil_kernel(q_ref, k_ref, v_ref, bias_ref, o_ref, m_scr, l_scr, acc_scr, *, nq):
    qi = pl.program_id(1)
    j = pl.program_id(2)
    kvb = qi + j - DIL_REACH
    C = HEAD_DIM

    @pl.when(j == 0)
    def _():
        m_scr[...] = jnp.full_like(m_scr, NEG)
        l_scr[...] = jnp.zeros_like(l_scr)
        acc_scr[...] = jnp.zeros_like(acc_scr)

    @pl.when((kvb >= 0) & (kvb < nq))
    def _():
        bias = bias_ref[j]
        for h in range(HEADS):
            sl = slice(h * C, (h + 1) * C)
            s = _dot_nt(q_ref[0, :, sl], k_ref[0, :, sl]) + bias
            m_prev = m_scr[h]
            m_new = jnp.maximum(m_prev, jnp.max(s, axis=-1, keepdims=True))
            alpha = jnp.exp(m_prev - m_new)
            p = jnp.exp(s - m_new)
            l_scr[h] = alpha * l_scr[h] + jnp.sum(p, axis=-1, keepdims=True)
            acc_scr[:, sl] = alpha * acc_scr[:, sl] + _dot(p.astype(BF16), v_ref[0, :, sl])
            m_scr[h] = m_new

    @pl.when(j == 2 * DIL_REACH)
    def _():
        for h in range(HEADS):
            sl = slice(h * C, (h + 1) * C)
            o_ref[0, :, sl] = acc_scr[:, sl] / l_scr[h]


def _dilated_attention(q, k, v, bias):
    B, S, W = q.shape
    t = DIL_TILE
    nq = S // t
    nj = 2 * DIL_REACH + 1
    qspec = pl.BlockSpec((1, t, W), lambda b, i, j: (b, i, 0))
    kspec = pl.BlockSpec((1, t, W), lambda b, i, j: (b, jnp.clip(i + j - DIL_REACH, 0, nq - 1), 0))
    bspec = pl.BlockSpec(bias.shape, lambda b, i, j: (0, 0, 0))
    return pl.pallas_call(
        functools.partial(_dil_kernel, nq=nq), grid=(B, nq, nj),
        in_specs=[qspec, kspec, kspec, bspec],
        out_specs=qspec, out_shape=jax.ShapeDtypeStruct((B, S, W), F32),
        scratch_shapes=[pltpu.VMEM((HEADS, t, 1), F32), pltpu.VMEM((HEADS, t, 1), F32), pltpu.VMEM((t, W), F32)],
        compiler_params=_params(("parallel", "parallel", "arbitrary")), name="dilated_attn",
    )(q, k, v, bias)


def _ssm_operators(lam_re, lam_im, log_dt, b_re, b_im, c_re, c_im):
    L, G, N, P = SSM_CHUNK, SSM_GROUPS, SSM_STATE, SSM_GROUP
    hp = lax.Precision.HIGHEST
    js = jnp.arange(L + 1, dtype=F32)[:, None, None]
    ops = []
    for d in range(2):
        dt = jnp.exp(log_dt[d])[:, None]
        mag = jnp.exp(lam_re[d] * dt)
        ar, ai = mag * jnp.cos(lam_im[d] * dt), mag * jnp.sin(lam_im[d] * dt)
        den = lam_re[d] * lam_re[d] + lam_im[d] * lam_im[d]
        fr = ((ar - 1.0) * lam_re[d] + ai * lam_im[d]) / den
        fi = (ai * lam_re[d] - (ar - 1.0) * lam_im[d]) / den
        bbr = fr[..., None] * b_re[d] - fi[..., None] * b_im[d]
        bbi = fr[..., None] * b_im[d] + fi[..., None] * b_re[d]
        pmag = jnp.exp(js * (lam_re[d] * dt)[None])
        pr = pmag * jnp.cos(js * (lam_im[d] * dt)[None])
        pi = pmag * jnp.sin(js * (lam_im[d] * dt)[None])
        xr = pr[..., None] * bbr[None] - pi[..., None] * bbi[None]
        xi = pr[..., None] * bbi[None] + pi[..., None] * bbr[None]
        kj = (jnp.einsum('gqn,jgnp->jgqp', c_re[d], xr[:L], precision=hp)
              - jnp.einsum('gqn,jgnp->jgqp', c_im[d], xi[:L], precision=hp))
        rr = jnp.arange(L)[:, None]
        ss = jnp.arange(L)[None, :]
        lag = (ss - rr) if d == 0 else (rr - ss)
        toe = jnp.where((lag >= 0)[:, :, None, None, None], kj[jnp.clip(lag, 0, L - 1)], 0.0)
        toe = toe.transpose(2, 0, 4, 1, 3).reshape(G, L * P, L * P)
        pw_state = (L - 1 - jnp.arange(L)) if d == 0 else jnp.arange(L)
        p_re = xr[pw_state].transpose(1, 0, 3, 2).reshape(G, L * P, N)
        p_im = xi[pw_state].transpose(1, 0, 3, 2).reshape(G, L * P, N)
        pw_out = (jnp.arange(L) + 1) if d == 0 else (L - jnp.arange(L))
        por, poi = pr[pw_out], pi[pw_out]
        cr, ci = c_re[d], c_im[d]
        q_re = (cr[None] * por[:, :, None, :] - ci[None] * poi[:, :, None, :])
        q_im = (-cr[None] * poi[:, :, None, :] - ci[None] * por[:, :, None, :])
        q_re = q_re.transpose(1, 3, 0, 2).reshape(G, N, L * P)
        q_im = q_im.transpose(1, 3, 0, 2).reshape(G, N, L * P)
        ops.append((toe, p_re, p_im, q_re, q_im, pr[L], pi[L]))
    f, b = ops
    t_sum = (f[0] + b[0]).astype(BF16)
    p_cat = jnp.concatenate([f[1], b[1], f[2], b[2]], axis=2).astype(BF16)
    q_cat = jnp.concatenate([f[3], b[3], f[4], b[4]], axis=1).astype(BF16)
    al_re = jnp.concatenate([f[5], b[5]], axis=1).reshape(1, G * 2 * N)
    al_im = jnp.concatenate([f[6], b[6]], axis=1).reshape(1, G * 2 * N)
    return t_sum, p_cat, q_cat, al_re, al_im


def _ssm_state_in_kernel(u_ref, p_ref, xre_ref, xim_ref):
    x = _dot(u_ref[0], p_ref[0])
    xre_ref[...] = x[:, 0:LANES]
    xim_ref[...] = x[:, LANES:2 * LANES]


def _ssm_state_in(u_t, p_cat):
    G, M, K = u_t.shape
    tm = min(M, 256)
    return pl.pallas_call(
        _ssm_state_in_kernel, grid=(G, M // tm),
        in_specs=[pl.BlockSpec((1, tm, K), lambda g, m: (g, m, 0)),
                  pl.BlockSpec((1, K, 2 * LANES), lambda g, m: (g, 0, 0))],
        out_specs=[pl.BlockSpec((tm, LANES), lambda g, m: (m, g))] * 2,
        out_shape=[jax.ShapeDtypeStruct((M, G * LANES), F32)] * 2,
        compiler_params=_params(("parallel", "parallel")), name="ssm_state_in",
    )(u_t, p_cat)


def _ssm_scan_kernel(xre_ref, xim_ref, are_ref, aim_ref, hfre_ref, hfim_ref, hbre_ref, hbim_ref):
    nc = xre_ref.shape[1]
    width = xre_ref.shape[2]
    ar = are_ref[...]
    ai = aim_ref[...]
    lane = lax.broadcasted_iota(jnp.int32, (1, width), 1)
    is_fwd = (lane % LANES) < SSM_STATE

    def step(i, carry):
        hr, hi = carry
        ib = nc - 1 - i
        hfre_ref[0, pl.ds(i, 1), :] = hr
        hfim_ref[0, pl.ds(i, 1), :] = hi
        hbre_ref[0, pl.ds(ib, 1), :] = hr
        hbim_ref[0, pl.ds(ib, 1), :] = hi
        xr = jnp.where(is_fwd, xre_ref[0, pl.ds(i, 1), :], xre_ref[0, pl.ds(ib, 1), :])
        xi = jnp.where(is_fwd, xim_ref[0, pl.ds(i, 1), :], xim_ref[0, pl.ds(ib, 1), :])
        return ar * hr - ai * hi + xr, ar * hi + ai * hr + xi

    zero = jnp.zeros((1, width), F32)
    lax.fori_loop(0, nc, step, (zero, zero))


def _ssm_scan(xre, xim, al_re, al_im):
    B, nc, width = xre.shape
    blk = pl.BlockSpec((1, nc, width), lambda b: (b, 0, 0))
    vec = pl.BlockSpec((1, width), lambda b: (0, 0))
    return pl.pallas_call(
        _ssm_scan_kernel, grid=(B,),
        in_specs=[blk, blk, vec, vec], out_specs=[blk] * 4,
        out_shape=[jax.ShapeDtypeStruct((B, nc, width), F32)] * 4,
        compiler_params=_params(("parallel",)), name="ssm_scan",
    )(xre, xim, al_re, al_im)


def _ssm_out_kernel(u_ref, t_ref, hfre_ref, hfim_ref, hbre_ref, hbim_ref, q_ref, y_ref):
    tm = u_ref.shape[1]
    lane = lax.broadcasted_iota(jnp.int32, (tm, LANES), 1)
    is_fwd = lane < SSM_STATE
    h_re = jnp.where(is_fwd, hfre_ref[...], hbre_ref[...])
    h_im = jnp.where(is_fwd, hfim_ref[...], hbim_ref[...])
    h_cat = jnp.concatenate([h_re, h_im], axis=1).astype(BF16)
    y_ref[0] = _dot(u_ref[0], t_ref[0]) + _dot(h_cat, q_ref[0])


def _ssm_out(u_t, t_sum, hs, q_cat):
    G, M, K = u_t.shape
    tm = min(M, 256)
    hblk = pl.BlockSpec((tm, LANES), lambda g, m: (m, g))
    return pl.pallas_call(
        _ssm_out_kernel, grid=(G, M // tm),
        in_specs=[pl.BlockSpec((1, tm, K), lambda g, m: (g, m, 0)),
                  pl.BlockSpec((1, K, K), lambda g, m: (g, 0, 0)),
                  hblk, hblk, hblk, hblk,
                  pl.BlockSpec((1, 2 * LANES, K), lambda g, m: (g, 0, 0))],
        out_specs=pl.BlockSpec((1, tm, K), lambda g, m: (g, m, 0)),
        out_shape=jax.ShapeDtypeStruct((G, M, K), F32),
        compiler_params=_params(("parallel", "parallel")), name="ssm_out",
    )(u_t, t_sum, *hs, q_cat)


def _s5_scan(u, ops):
    t_sum, p_cat, q_cat, al_re, al_im = ops
    B, S, _ = u.shape
    L, G, P = SSM_CHUNK, SSM_GROUPS, SSM_GROUP
    nc = S // L
    u_t = u.reshape(B, nc, L, G, P).transpose(3, 0, 1, 2, 4).reshape(G, B * nc, L * P).astype(BF16)
    xre, xim = _ssm_state_in(u_t, p_cat)
    width = G * LANES
    hs = _ssm_scan(xre.reshape(B, nc, width), xim.reshape(B, nc, width), al_re, al_im)
    hfre, hfim, hbre, hbim = (h.reshape(B * nc, width) for h in hs)
    y = _ssm_out(u_t, t_sum, (hfre, hfim, hbre, hbim), q_cat)
    return y.reshape(G, B, nc, L, P).transpose(1, 2, 3, 0, 4).reshape(B, S, G * P)


def _na_bias_table(rpb):
    wr = NA_WIN_R
    c = jnp.arange(GRID_W)
    cstart = jnp.clip(c - NA_WIN_C // 2, 0, GRID_W - NA_WIN_C)
    col_ok = (c[None, :] >= cstart[:, None]) & (c[None, :] < cstart[:, None] + NA_WIN_C)
    off = jnp.arange(wr)
    dr = jnp.arange(wr)[None, :] - off[:, None] + (NA_WIN_R - 1)
    dc = jnp.clip(c[None, :] - c[:, None] + (NA_WIN_C - 1), 0, 2 * NA_WIN_C - 2)
    tbl = rpb.astype(F32)[:, dr[:, None, :, None], dc[None, :, None, :]]
    tbl = jnp.where(col_ok[None, None, :, None, :], tbl, NEG)
    return tbl.reshape(HEADS, wr, GRID_W, wr * GRID_W)


def _na_kernel(q_ref, kp_ref, kc_ref, kn_ref, vp_ref, vc_ref, vn_ref, bias_ref, o_ref, kbuf, vbuf, *, rows):
    rb = pl.program_id(1)
    Wg = GRID_W
    C = HEAD_DIM
    halo = NA_HALO_ROWS * Wg
    cur = NA_ROWS_PER_STEP * Wg
    win = NA_WIN_R * Wg
    kbuf[0:halo, :] = kp_ref[0]
    kbuf[halo:halo + cur, :] = kc_ref[0]
    kbuf[halo + cur:2 * halo + cur, :] = kn_ref[0]
    vbuf[0:halo, :] = vp_ref[0]
    vbuf[halo:halo + cur, :] = vc_ref[0]
    vbuf[halo + cur:2 * halo + cur, :] = vn_ref[0]
    r0 = rb * NA_ROWS_PER_STEP
    for j in range(NA_ROWS_PER_STEP):
        r = r0 + j
        rs = jnp.clip(r - NA_WIN_R // 2, 0, rows - NA_WIN_R)
        off = r - rs
        start = pl.multiple_of((rs - r0 + NA_HALO_ROWS) * Wg, Wg)
        for h in range(HEADS):
            sl = slice(h * C, (h + 1) * C)
            qh = q_ref[0, j * Wg:(j + 1) * Wg, sl]
            kw = kbuf[pl.ds(start, win), sl]
            vw = vbuf[pl.ds(start, win), sl]
            s = _dot_nt(qh, kw) + bias_ref[h, off]
            m = jnp.max(s, axis=-1, keepdims=True)
            p = jnp.exp(s - m)
            l = jnp.sum(p, axis=-1, keepdims=True)
            o_ref[0, j * Wg:(j + 1) * Wg, sl] = _dot(p.astype(BF16), vw) / l


def _neighbourhood_attention(q, k, v, bias):
    B, S, W = q.shape
    rows = S // GRID_W
    cur = NA_ROWS_PER_STEP * GRID_W
    halo = NA_HALO_ROWS * GRID_W
    nrb = S // cur
    per = cur // halo
    nh = S // halo
    blk = pl.BlockSpec((1, cur, W), lambda b, i: (b, i, 0))
    prev = pl.BlockSpec((1, halo, W), lambda b, i: (b, jnp.maximum(i * per - 1, 0), 0))
    nxt = pl.BlockSpec((1, halo, W), lambda b, i: (b, jnp.minimum((i + 1) * per, nh - 1), 0))
    bspec = pl.BlockSpec(bias.shape, lambda b, i: (0, 0, 0, 0))
    return pl.pallas_call(
        functools.partial(_na_kernel, rows=rows), grid=(B, nrb),
        in_specs=[blk, prev, blk, nxt, prev, blk, nxt, bspec],
        out_specs=blk, out_shape=jax.ShapeDtypeStruct((B, S, W), F32),
        scratch_shapes=[pltpu.VMEM((cur + 2 * halo, W), BF16), pltpu.VMEM((cur + 2 * halo, W), BF16)],
        compiler_params=_params(("parallel", "parallel")), name="neighbourhood_attn",
    )(q, k, k, k, v, v, v, bias)


def _post_kernel(x_ref, of_ref, ob_ref, gate_ref, yb_ref, ys_ref, u_ref, yd_ref,
                 dnw_ref, bd_ref, dskip_ref, gluw_ref, glub_ref, wout_ref, ln2_ref, w1_ref, w2_ref, fnw_ref,
                 out_ref, *, final):
    x = x_ref[...]
    o = of_ref[...] + ob_ref[...]
    ms = _head_sums(o * o, bd_ref[...]) * (1.0 / HEAD_DIM)
    g = gate_ref[...]
    y_a = o * lax.rsqrt(ms + EPS) * dnw_ref[...] * (g * _sigmoid(g))
    y = ys_ref[...] + dskip_ref[...] * u_ref[...]
    z = 0.5 * y * (1.0 + jnp.tanh(math.sqrt(2.0 / math.pi) * (y + 0.044715 * (y * y * y))))
    y_c = z * _sigmoid(_dot(z.astype(BF16), gluw_ref[...]) + glub_ref[...])
    mix = jnp.concatenate([y_a, yb_ref[...], y_c, yd_ref[...]], axis=1).astype(BF16)
    x1 = x + _dot(mix, wout_ref[...])
    hb = _rms(x1, ln2_ref[...]).astype(BF16)
    ffc = D_MODEL
    acc = x1
    for c in range(D_FF // ffc):
        a = jnp.maximum(_dot(hb, w1_ref[:, c * ffc:(c + 1) * ffc]), 0.0)
        acc = acc + _dot((a * a).astype(BF16), w2_ref[c * ffc:(c + 1) * ffc, :])
    if final:
        acc = _rms(acc, fnw_ref[...])
    out_ref[...] = acc


def _post(x, o_f, o_b, gate, y_b, y_s, u, y_d, wts, final):
    T = x.shape[0]
    tm = min(TOKEN_TILE, T)
    W = WIDTH
    row = lambda n: pl.BlockSpec((tm, n), lambda i: (i, 0))
    const = lambda a: pl.BlockSpec(a.shape, lambda i: (0, 0), pipeline_mode=pl.Buffered(1))
    return pl.pallas_call(
        functools.partial(_post_kernel, final=final), grid=(T // tm,),
        in_specs=[row(D_MODEL)] + [row(W)] * 7 + [const(a) for a in wts],
        out_specs=row(D_MODEL), out_shape=jax.ShapeDtypeStruct((T, D_MODEL), F32),
        compiler_params=_params(("parallel",)), name="post",
    )(x, o_f, o_b, gate, y_b, y_s, u, y_d, *wts)


def _rope_tables(seq):
    inv_freq = ROPE_THETA ** (-jnp.arange(0, HEAD_DIM, 2, dtype=F32) / HEAD_DIM)
    ang = jnp.arange(seq, dtype=F32)[:, None] * inv_freq[None, :]
    cos, sin = jnp.cos(ang), jnp.sin(ang)
    cos_t = jnp.tile(jnp.concatenate([cos, cos], axis=1), (1, HEADS))
    sin_t = jnp.tile(jnp.concatenate([-sin, sin], axis=1), (1, HEADS))
    return cos_t, sin_t


def _layer_weights(i, p):
    W = WIDTH
    w = p['w_in'][i]
    o_gate, o_a, o_b, o_dil, o_ssm, o_na = 3 * W, 4 * W, 4 * W + 2 * HEADS, 4 * W + 4 * HEADS, 7 * W + 4 * HEADS, 8 * W + 4 * HEADS
    pad = jnp.zeros((D_MODEL, LANES - 4 * HEADS), F32)
    w_in = jnp.concatenate([w[:, 0:o_a], w[:, o_dil:o_na + 3 * W], w[:, o_a:o_dil], pad], axis=1).astype(BF16)
    assert w_in.shape[1] == IN_COLS_PADDED
    lane_pad = lambda v: jnp.concatenate([v.reshape(1, -1).astype(F32), jnp.zeros((1, LANES - v.size), F32)], axis=1)
    idx = np.arange(W)
    bd = jnp.asarray((idx[:, None] // HEAD_DIM) == (idx[None, :] // HEAD_DIM), BF16)
    row = lambda v: v.reshape(1, -1).astype(F32)
    return dict(
        ln1=row(p['ln1_w'][i]), w_in=w_in,
        conv_w=p['dn_conv_w'][i].astype(F32), alog=lane_pad(p['dn_a_log'][i]), dtb=lane_pad(p['dn_dt_bias'][i]),
        bd=bd,
        ssm=_ssm_operators(*(p[n][i].astype(F32) for n in
                             ('ssm_lam_re', 'ssm_lam_im', 'ssm_log_dt', 'ssm_b_re', 'ssm_b_im', 'ssm_c_re', 'ssm_c_im'))),
        na_bias=_na_bias_table(p['na_rpb'][i]),
        post=(row(jnp.tile(p['dn_norm_w'][i], HEADS)), bd, row(p['ssm_d'][i]), p['ssm_glu_w'][i].astype(BF16),
              row(p['ssm_glu_b'][i]), p['w_out'][i].astype(BF16), row(p['ln2_w'][i]),
              p['w_ff1'][i].astype(BF16), p['w_ff2'][i].astype(BF16), row(p['final_norm_w'])),
    )


def _trunk(x, layers, dil_bias):
    B, S, D = x.shape
    T = B * S
    cos_t, sin_t = _rope_tables(S)
    xt = x.reshape(T, D)
    seq3 = lambda a: a.reshape(B, S, a.shape[-1])
    flat = lambda a: a.reshape(T, a.shape[-1])
    for i, lw in enumerate(layers):
        z_dn, z_gate, z_ab, dq, dk, dv, z_ssm, nq, nk, nv = _in_proj(xt, lw['ln1'], lw['w_in'], cos_t, sin_t, S)
        q, k, v, gb = _dn_prep(seq3(z_dn), seq3(z_ab), lw['conv_w'], lw['alog'], lw['dtb'], lw['bd'])
        o_f = _dn_scan(q, k, v, gb, False)
        o_b = _dn_scan(q, k, v, gb, True)
        y_b = _dilated_attention(seq3(dq), seq3(dk), seq3(dv), dil_bias)
        y_s = _s5_scan(seq3(z_ssm), lw['ssm'])
        y_d = _neighbourhood_attention(seq3(nq), seq3(nk), seq3(nv), lw['na_bias'])
        xt = _post(xt, flat(o_f), flat(o_b), z_gate, flat(y_b), flat(y_s), z_ssm, flat(y_d),
                   lw['post'], final=(i == len(layers) - 1))
    return xt.reshape(B, S, D)


def kernel(x_prompt, x_sample, ln1_w, w_in, dn_conv_w, dn_a_log, dn_dt_bias, dn_norm_w, ssm_lam_re, ssm_lam_im,
           ssm_log_dt, ssm_b_re, ssm_b_im, ssm_c_re, ssm_c_im, ssm_d, ssm_glu_w, ssm_glu_b, na_rpb, w_out,
           ln2_w, w_ff1, w_ff2, final_norm_w):
    p = dict(ln1_w=ln1_w, w_in=w_in, dn_conv_w=dn_conv_w, dn_a_log=dn_a_log, dn_dt_bias=dn_dt_bias,
             dn_norm_w=dn_norm_w, ssm_lam_re=ssm_lam_re, ssm_lam_im=ssm_lam_im, ssm_log_dt=ssm_log_dt,
             ssm_b_re=ssm_b_re, ssm_b_im=ssm_b_im, ssm_c_re=ssm_c_re, ssm_c_im=ssm_c_im, ssm_d=ssm_d,
             ssm_glu_w=ssm_glu_w, ssm_glu_b=ssm_glu_b, na_rpb=na_rpb, w_out=w_out, ln2_w=ln2_w,
             w_ff1=w_ff1, w_ff2=w_ff2, final_norm_w=final_norm_w)
    layers = [_layer_weights(i, p) for i in range(DEPTH)]
    dil_bias = jnp.asarray(_dil_bias_tiles())
    return (_trunk(x_prompt, layers, dil_bias), _trunk(x_sample, layers, dil_bias))
```

```python
import functools
import math

import numpy as np
import jax
import jax.numpy as jnp
from jax import lax
from jax.experimental import pallas as pl
from jax.experimental.pallas import tpu as pltpu

F32 = jnp.float32
BF16 = jnp.bfloat16

D_MODEL = 1024
DEPTH = 2
HEAD_DIM = 64
HEADS = 4
WIDTH = HEADS * HEAD_DIM
DN_CHUNK = 64
CONV_WIDTH = 5
DIL_PATTERNS = ((128, 1), (512, 4), (2048, 16))
SSM_GROUP = 16
SSM_GROUPS = 16
SSM_STATE = 64
SSM_CHUNK = 64
GRID_W = 64
NA_WIN_R = 8
NA_WIN_C = 16
D_FF = 4 * D_MODEL
ROPE_THETA = 10000.0
EPS = 1e-6
NEG = -1e30

LANES = 128
TOKEN_TILE = 512
DIL_TILE = 512
DIL_REACH = 2
NA_ROWS_PER_STEP = 8
NA_HALO_ROWS = 4
IN_COLS_PADDED = 23 * LANES
VMEM_LIMIT = 56 * 2**20


def _params(sem):
    return pltpu.CompilerParams(dimension_semantics=sem, vmem_limit_bytes=VMEM_LIMIT)


def _dot(a, b):
    return jnp.dot(a, b, preferred_element_type=F32)


def _dot_nt(a, b):
    return lax.dot_general(a, b, (((1,), (1,)), ((), ())), preferred_element_type=F32)


def _bmm(a, b):
    return lax.dot_general(a, b, (((2,), (1,)), ((0,), (0,))), preferred_element_type=F32)


def _bmm_nt(a, b):
    return lax.dot_general(a, b, (((2,), (2,)), ((0,), (0,))), preferred_element_type=F32)


def _split2(x):
    hi = x.astype(BF16)
    lo = (x - hi.astype(F32)).astype(BF16)
    return hi, lo


def _split3(x):
    hi = x.astype(BF16)
    r = x - hi.astype(F32)
    mid = r.astype(BF16)
    lo = (r - mid.astype(F32)).astype(BF16)
    return hi, mid, lo


def _exact_lhs_dot(m, x):
    hi, mid, lo = _split3(x)
    return _dot(m, hi) + _dot(m, mid) + _dot(m, lo)


def _head_sums(sq, bd):
    hi, lo = _split2(sq)
    return _dot(hi, bd) + _dot(lo, bd)


def _sigmoid(x):
    return 1.0 / (1.0 + jnp.exp(-x))


def _rms(x, w):
    ms = jnp.mean(x * x, axis=-1, keepdims=True)
    return x * lax.rsqrt(ms + EPS) * w


def _in_proj_kernel(x_ref, lnw_ref, w_ref, cos_ref, sin_ref,
                    dn_ref, gate_ref, ab_ref, dq_ref, dk_ref, dv_ref, ssm_ref, nq_ref, nk_ref, nv_ref):
    h = _rms(x_ref[...], lnw_ref[...]).astype(BF16)
    z = _dot(h, w_ref[...])
    tm = z.shape[0]
    W = WIDTH
    dn_ref[...] = z[:, 0:3 * W]
    gate_ref[...] = z[:, 3 * W:4 * W]
    cos = cos_ref[...]
    sin = sin_ref[...]
    lane = lax.broadcasted_iota(jnp.int32, (tm, W), 1)
    first_half = (lane % HEAD_DIM) < (HEAD_DIM // 2)

    def rope(t):
        rot = jnp.where(first_half, pltpu.roll(t, W - HEAD_DIM // 2, 1), pltpu.roll(t, HEAD_DIM // 2, 1))
        return t * cos + rot * sin

    scale = HEAD_DIM ** -0.5
    dq_ref[...] = (rope(z[:, 4 * W:5 * W]) * scale).astype(BF16)
    dk_ref[...] = rope(z[:, 5 * W:6 * W]).astype(BF16)
    dv_ref[...] = z[:, 6 * W:7 * W].astype(BF16)
    ssm_ref[...] = z[:, 7 * W:8 * W]
    nq_ref[...] = (z[:, 8 * W:9 * W] * scale).astype(BF16)
    nk_ref[...] = z[:, 9 * W:10 * W].astype(BF16)
    nv_ref[...] = z[:, 10 * W:11 * W].astype(BF16)
    ab_ref[...] = z[:, 11 * W:11 * W + LANES]


def _in_proj(x, lnw, w, cos_t, sin_t, seq):
    T = x.shape[0]
    tm = min(TOKEN_TILE, seq)
    per_seq = seq // tm
    W = WIDTH
    row = lambda n: pl.BlockSpec((tm, n), lambda i: (i, 0))
    const = lambda a: pl.BlockSpec(a.shape, lambda i: (0, 0))
    tab = pl.BlockSpec((tm, W), lambda i: (i % per_seq, 0))
    out_shapes = [
        jax.ShapeDtypeStruct((T, 3 * W), F32), jax.ShapeDtypeStruct((T, W), F32),
        jax.ShapeDtypeStruct((T, LANES), F32),
        jax.ShapeDtypeStruct((T, W), BF16), jax.ShapeDtypeStruct((T, W), BF16), jax.ShapeDtypeStruct((T, W), BF16),
        jax.ShapeDtypeStruct((T, W), F32),
        jax.ShapeDtypeStruct((T, W), BF16), jax.ShapeDtypeStruct((T, W), BF16), jax.ShapeDtypeStruct((T, W), BF16),
    ]
    out_specs = [row(3 * W), row(W), row(LANES), row(W), row(W), row(W), row(W), row(W), row(W), row(W)]
    return pl.pallas_call(
        _in_proj_kernel, grid=(T // tm,),
        in_specs=[row(D_MODEL), const(lnw), const(w), tab, tab],
        out_specs=out_specs, out_shape=out_shapes,
        compiler_params=_params(("parallel",)), name="in_proj",
    )(x, lnw, w, cos_t, sin_t)


def _dn_prep_kernel(zc_ref, zp_ref, zn_ref, ab_ref, cw_ref, alog_ref, dtb_ref, bd_ref,
                    q_ref, k_ref, v_ref, gb_ref, buf):
    i = pl.program_id(1)
    n = pl.num_programs(1)
    tb = zc_ref.shape[1]
    W = WIDTH
    half = CONV_WIDTH // 2
    buf[0:8, :] = jnp.where(i > 0, zp_ref[0], 0.0)
    buf[8:8 + tb, :] = zc_ref[0]
    buf[8 + tb:16 + tb, :] = jnp.where(i < n - 1, zn_ref[0], 0.0)
    cw = cw_ref[...]
    acc = buf[8 - half:8 - half + tb, :] * cw[0:1, :]
    for j in range(1, CONV_WIDTH):
        acc = acc + buf[8 - half + j:8 - half + j + tb, :] * cw[j:j + 1, :]
    y = acc * _sigmoid(acc)
    bd = bd_ref[...]
    q = y[:, 0:W]
    k = y[:, W:2 * W]
    q_ref[0] = q * lax.rsqrt(_head_sums(q * q, bd) + EPS) * (HEAD_DIM ** -0.5)
    k_ref[0] = k * lax.rsqrt(_head_sums(k * k, bd) + EPS)
    v_ref[0] = y[:, 2 * W:3 * W]
    ab = ab_ref[0]
    t = ab + dtb_ref[...]
    softplus = jnp.maximum(t, 0.0) + jnp.log(1.0 + jnp.exp(-jnp.abs(t)))
    g = -jnp.exp(alog_ref[...]) * softplus
    lane = lax.broadcasted_iota(jnp.int32, ab.shape, 1)
    gb_ref[0] = jnp.where(lane < 2 * HEADS, g, _sigmoid(ab))


def _dn_prep(z_dn, z_ab, conv_w, alog, dtb, bd):
    B, S, C = z_dn.shape
    tb = min(TOKEN_TILE, S)
    nb = S // tb
    r8 = tb // 8
    W = WIDTH
    blk = lambda n: pl.BlockSpec((1, tb, n), lambda b, i: (b, i, 0))
    const = lambda a: pl.BlockSpec(a.shape, lambda b, i: (0, 0))
    prev = pl.BlockSpec((1, 8, C), lambda b, i: (b, jnp.maximum(i * r8 - 1, 0), 0))
    nxt = pl.BlockSpec((1, 8, C), lambda b, i: (b, jnp.minimum((i + 1) * r8, S // 8 - 1), 0))
    return pl.pallas_call(
        _dn_prep_kernel, grid=(B, nb),
        in_specs=[blk(C), prev, nxt, blk(LANES), const(conv_w), const(alog), const(dtb), const(bd)],
        out_specs=[blk(W), blk(W), blk(W), blk(LANES)],
        out_shape=[jax.ShapeDtypeStruct((B, S, W), F32)] * 3 + [jax.ShapeDtypeStruct((B, S, LANES), F32)],
        scratch_shapes=[pltpu.VMEM((tb + 16, C), F32)],
        compiler_params=_params(("parallel", "parallel")), name="dn_prep",
    )(z_dn, z_dn, z_dn, z_ab, conv_w, alog, dtb, bd)


def _dn_scan_kernel(q_ref, k_ref, v_ref, gb_ref, o_ref,
                    s_scr, u_scr, w_scr, qg_scr, at_scr, kdt_scr, *, rev, nc):
    C = DN_CHUNK
    tb = nc * C
    H = HEADS

    @pl.when(pl.program_id(1) == 0)
    def _():
        s_scr[...] = jnp.zeros_like(s_scr)

    gb = gb_ref[0]
    r = lax.broadcasted_iota(jnp.int32, (tb, tb), 0)
    c = lax.broadcasted_iota(jnp.int32, (tb, tb), 1)
    same = (r // C) == (c // C)
    cum = same & ((c >= r) if rev else (c <= r))
    gam = _exact_lhs_dot(jnp.where(cum, 1.0, 0.0).astype(BF16), gb)
    gtot = _exact_lhs_dot(jnp.where(same, 1.0, 0.0).astype(BF16), gb)
    ri = lax.broadcasted_iota(jnp.int32, (LANES, LANES), 0)
    ci = lax.broadcasted_iota(jnp.int32, (LANES, LANES), 1)
    eye128 = jnp.where(ri == ci, 1.0, 0.0).astype(BF16)
    g_hi, g_mid, g_lo = _split3(gam)
    gam_t = _dot_nt(eye128, g_hi) + _dot_nt(eye128, g_mid) + _dot_nt(eye128, g_lo)
    eye64 = eye128[0:C, 0:C]

    ii = lax.broadcasted_iota(jnp.int32, (tb, C), 0) % C
    jj = lax.broadcasted_iota(jnp.int32, (tb, C), 1)
    incl = (jj >= ii) if rev else (jj <= ii)
    strict = (jj > ii) if rev else (jj < ii)
    eye_f = jnp.where(ii == jj, 1.0, 0.0)

    for h in range(H):
        ch = (H if rev else 0) + h
        gam_c = gam[:, ch:ch + 1]
        gtot_c = gtot[:, ch:ch + 1]
        beta = gb[:, 2 * H + ch:2 * H + ch + 1]
        grow = gam_t[ch:ch + 1, :]
        g_rows = jnp.concatenate(
            [jnp.broadcast_to(grow[:, cc * C:(cc + 1) * C], (C, C)) for cc in range(nc)], axis=0)
        dec = jnp.exp(jnp.where(incl, gam_c - g_rows, NEG))
        qh = q_ref[0, :, h * C:(h + 1) * C]
        kh = k_ref[0, :, h * C:(h + 1) * C]
        vh = v_ref[0, :, h * C:(h + 1) * C]
        kb = kh * beta
        k3 = kh.astype(BF16).reshape(nc, C, C)
        a_raw = _bmm_nt(kb.astype(BF16).reshape(nc, C, C), k3).reshape(tb, C)
        qk = _bmm_nt(qh.astype(BF16).reshape(nc, C, C), k3).reshape(tb, C)
        lm = jnp.where(strict, a_raw * dec, 0.0)
        attn = qk * dec
        t_inv = eye_f - jnp.where((ii // 2) == (jj // 2), lm, 0.0)
        n = 2
        while n < C:
            off = ((ii // (2 * n)) == (jj // (2 * n))) & ((ii // n) != (jj // n))
            bn = jnp.where(off, lm, 0.0).astype(BF16).reshape(nc, C, C)
            t3 = t_inv.astype(BF16).reshape(nc, C, C)
            bt = _bmm(bn, t3)
            t_inv = t_inv - _bmm(t3, bt.astype(BF16)).reshape(tb, C)
            n *= 2
        eg = jnp.exp(gam_c)
        rhs = jnp.concatenate([vh * beta, kb * eg], axis=1)
        t_hi, t_lo = _split2(t_inv)
        r_hi, r_lo = _split2(rhs)
        t_hi3 = t_hi.reshape(nc, C, C)
        sol = (_bmm(t_hi3, r_hi.reshape(nc, C, 2 * C)) + _bmm(t_hi3, r_lo.reshape(nc, C, 2 * C))
               + _bmm(t_lo.reshape(nc, C, C), r_hi.reshape(nc, C, 2 * C))).reshape(tb, 2 * C)
        u_scr[h] = sol[:, 0:C]
        w_scr[h] = sol[:, C:2 * C].astype(BF16)
        qg_scr[h] = (qh * eg).astype(BF16)
        at_scr[h] = attn.astype(BF16)
        kd = kh * jnp.exp(gtot_c - gam_c)
        kdt_scr[h] = _dot_nt(eye64, kd.astype(BF16)).astype(BF16)

    order = range(nc - 1, -1, -1) if rev else range(nc)
    for cc in order:
        lo, hi = cc * C, (cc + 1) * C
        for h in range(H):
            ch = (H if rev else 0) + h
            s = s_scr[h]
            sb = s.astype(BF16)
            v_new = u_scr[h, lo:hi, :] - _dot(w_scr[h, lo:hi, :], sb)
            vb = v_new.astype(BF16)
            o = _dot(qg_scr[h, lo:hi, :], sb) + _dot(at_scr[h, lo:hi, :], vb)
            o_ref[0, lo:hi, h * C:(h + 1) * C] = o
            gl = jnp.exp(gtot[lo:lo + 1, ch:ch + 1])
            s_scr[h] = s * gl + _dot(kdt_scr[h, :, lo:hi], vb)


def _dn_scan(q, k, v, gb, rev):
    B, S, W = q.shape
    tb = min(TOKEN_TILE, S)
    nb = S // tb
    nc = tb // DN_CHUNK
    C = DN_CHUNK
    if rev:
        imap = lambda b, i: (b, nb - 1 - i, 0)
    else:
        imap = lambda b, i: (b, i, 0)
    blk = lambda n: pl.BlockSpec((1, tb, n), imap)
    return pl.pallas_call(
        functools.partial(_dn_scan_kernel, rev=rev, nc=nc), grid=(B, nb),
        in_specs=[blk(W), blk(W), blk(W), blk(LANES)],
        out_specs=blk(W), out_shape=jax.ShapeDtypeStruct((B, S, W), F32),
        scratch_shapes=[pltpu.VMEM((HEADS, C, C), F32), pltpu.VMEM((HEADS, tb, C), F32),
                        pltpu.VMEM((HEADS, tb, C), BF16), pltpu.VMEM((HEADS, tb, C), BF16),
                        pltpu.VMEM((HEADS, tb, C), BF16), pltpu.VMEM((HEADS, C, tb), BF16)],
        compiler_params=_params(("parallel", "arbitrary")), name="dn_scan_bwd" if rev else "dn_scan_fwd",
    )(q, k, v, gb)


def _dil_bias_tiles():
    t = DIL_TILE
    qq = np.arange(t)[:, None]
    kk = np.arange(t)[None, :]
    tiles = []
    for o in range(-DIL_REACH, DIL_REACH + 1):
        delta = o * t + kk - qq
        cnt = np.zeros((t, t), np.float64)
        for window, dil in DIL_PATTERNS:
            cnt += ((delta % dil) == 0) & (np.abs(delta) <= window // 2)
        tiles.append(np.where(cnt > 0, np.log(np.maximum(cnt, 1.0)), NEG))
    return np.stack(tiles).astype(np.float32)


def _dil_kernel(q_ref, k_ref, v_ref, bias_ref, o_ref, m_scr, l_scr, acc_scr, *, nq):
    qi = pl.program_id(1)
    j = pl.program_id(2)
    kvb = qi + j - DIL_REACH
    C = HEAD_DIM

    @pl.when(j == 0)
    def _():
        m_scr[...] = jnp.full_like(m_scr, NEG)
        l_scr[...] = jnp.zeros_like(l_scr)
        acc_scr[...] = jnp.zeros_like(acc_scr)

    @pl.when((kvb >= 0) & (kvb < nq))
    def _():
        bias = bias_ref[j]
        for h in range(HEADS):
            sl = slice(h * C, (h + 1) * C)
            s = _dot_nt(q_ref[0, :, sl], k_ref[0, :, sl]) + bias
            m_prev = m_scr[h]
            m_new = jnp.maximum(m_prev, jnp.max(s, axis=-1, keepdims=True))
            alpha = jnp.exp(m_prev - m_new)
            p = jnp.exp(s - m_new)
            l_scr[h] = alpha * l_scr[h] + jnp.sum(p, axis=-1, keepdims=True)
            acc_scr[:, sl] = alpha * acc_scr[:, sl] + _dot(p.astype(BF16), v_ref[0, :, sl])
            m_scr[h] = m_new

    @pl.when(j == 2 * DIL_REACH)
    def _():
        for h in range(HEADS):
            sl = slice(h * C, (h + 1) * C)
            o_ref[0, :, sl] = acc_scr[:, sl] / l_scr[h]


def _dilated_attention(q, k, v, bias):
    B, S, W = q.shape
    t = DIL_TILE
    nq = S // t
    nj = 2 * DIL_REACH + 1
    qspec = pl.BlockSpec((1, t, W), lambda b, i, j: (b, i, 0))
    kspec = pl.BlockSpec((1, t, W), lambda b, i, j: (b, jnp.clip(i + j - DIL_REACH, 0, nq - 1), 0))
    bspec = pl.BlockSpec(bias.shape, lambda b, i, j: (0, 0, 0))
    return pl.pallas_call(
        functools.partial(_dil_kernel, nq=nq), grid=(B, nq, nj),
        in_specs=[qspec, kspec, kspec, bspec],
        out_specs=qspec, out_shape=jax.ShapeDtypeStruct((B, S, W), F32),
        scratch_shapes=[pltpu.VMEM((HEADS, t, 1), F32), pltpu.VMEM((HEADS, t, 1), F32), pltpu.VMEM((t, W), F32)],
        compiler_params=_params(("parallel", "parallel", "arbitrary")), name="dilated_attn",
    )(q, k, v, bias)


def _dot2(a, b):
    a_hi, a_lo = _split2(a)
    b_hi, b_lo = _split2(b)
    return _dot(a_hi, b_hi) + _dot(a_hi, b_lo) + _dot(a_lo, b_hi)


def _exact_rhs_dot(x, m):
    hi, mid, lo = _split3(x)
    return _dot(hi, m) + _dot(mid, m) + _dot(lo, m)


def _ssm_ops_kernel(lamr_row, lami_row, ldt_row, lamr_col, lami_col, ldt_col, brt_ref, bit_ref, crt_ref, cit_ref,
                    t_ref, p_ref, q_ref, al_ref):
    L, N, P = SSM_CHUNK, SSM_STATE, SSM_GROUP
    K = L * P
    lane_j = lax.broadcasted_iota(jnp.int32, (L, K), 1) // P
    row_j = lax.broadcasted_iota(jnp.int32, (L, K), 0)
    sel_r = lax.broadcasted_iota(jnp.int32, (K, L), 0) // P
    sel_j = lax.broadcasted_iota(jnp.int32, (K, L), 1)
    jcol = lax.broadcasted_iota(jnp.int32, (N, L), 1).astype(F32)
    jrow = lax.broadcasted_iota(jnp.int32, (L, N), 0).astype(F32)
    kcat, p_parts, q_parts, al_parts = [], [], [], []
    for d in range(2):
        expand = jnp.where(row_j == (lane_j if d == 0 else L - 1 - lane_j), 1.0, 0.0).astype(BF16)
        select = jnp.where(sel_j == (L - 1 - sel_r if d == 0 else sel_r), 1.0, 0.0).astype(BF16)
        lr_c, li_c, dt_c = lamr_col[d, 0], lami_col[d, 0], jnp.exp(ldt_col[d, 0])
        pmag = jnp.exp(jcol * (lr_c * dt_c))
        pr0 = _exact_rhs_dot(pmag * jnp.cos(jcol * (li_c * dt_c)), expand)
        pi0 = _exact_rhs_dot(pmag * jnp.sin(jcol * (li_c * dt_c)), expand)
        mag_c = jnp.exp(lr_c * dt_c)
        ar_c, ai_c = mag_c * jnp.cos(li_c * dt_c), mag_c * jnp.sin(li_c * dt_c)
        pr1, pi1 = ar_c * pr0 - ai_c * pi0, ar_c * pi0 + ai_c * pr0
        cr = jnp.concatenate([crt_ref[d, 0]] * (K // LANES), axis=1)
        ci = jnp.concatenate([cit_ref[d, 0]] * (K // LANES), axis=1)
        lr_r, li_r, dt_r = lamr_row[d, 0], lami_row[d, 0], jnp.exp(ldt_row[d, 0])
        mag_r = jnp.exp(lr_r * dt_r)
        ar_r, ai_r = mag_r * jnp.cos(li_r * dt_r), mag_r * jnp.sin(li_r * dt_r)
        den = lr_r * lr_r + li_r * li_r
        fr = ((ar_r - 1.0) * lr_r + ai_r * li_r) / den
        fi = (ai_r * lr_r - (ar_r - 1.0) * li_r) / den
        bbr = fr * brt_ref[d, 0] - fi * bit_ref[d, 0]
        bbi = fr * bit_ref[d, 0] + fi * brt_ref[d, 0]
        kcat.append(_dot2(bbr, pr0 * cr - pi0 * ci) - _dot2(bbi, pr0 * ci + pi0 * cr))
        q_parts.append((pr1 * cr - pi1 * ci, -(pi1 * cr + pr1 * ci)))
        rmag = jnp.exp(jrow * (lr_r * dt_r))
        prr = rmag * jnp.cos(jrow * (li_r * dt_r))
        pir = rmag * jnp.sin(jrow * (li_r * dt_r))
        prs, pis = _exact_lhs_dot(select, prr), _exact_lhs_dot(select, pir)
        bbr_t = jnp.concatenate([bbr] * L, axis=0)
        bbi_t = jnp.concatenate([bbi] * L, axis=0)
        p_parts.append((prs * bbr_t - pis * bbi_t, prs * bbi_t + pis * bbr_t))
        al_parts.append((ar_r * prr[L - 1:L] - ai_r * pir[L - 1:L], ar_r * pir[L - 1:L] + ai_r * prr[L - 1:L]))
    zeros = jnp.zeros((P, K), F32)
    ksum = (jnp.concatenate([zeros, kcat[0]], axis=1)
            + pltpu.roll(jnp.concatenate([kcat[1], zeros], axis=1), P, 1))
    per_tile = LANES // P
    for b in range(per_tile):
        rolled = ksum if b == 0 else pltpu.roll(ksum, P * b, 1)
        for a in range(L // per_tile):
            r = per_tile * a + b
            t_ref[0, P * r:P * (r + 1), :] = rolled[:, K - LANES * a:2 * K - LANES * a].astype(BF16)
    p_ref[0] = jnp.concatenate([p_parts[0][0], p_parts[1][0], p_parts[0][1], p_parts[1][1]], axis=1).astype(BF16)
    q_ref[0] = jnp.concatenate([q_parts[0][0], q_parts[1][0], q_parts[0][1], q_parts[1][1]], axis=0).astype(BF16)
    al_ref[0] = jnp.concatenate([jnp.concatenate([al_parts[0][0], al_parts[1][0]], axis=1),
                                 jnp.concatenate([al_parts[0][1], al_parts[1][1]], axis=1)], axis=0)


def _ssm_operators(lam_re, lam_im, log_dt, b_re, b_im, c_re, c_im):
    L, G, N, P = SSM_CHUNK, SSM_GROUPS, SSM_STATE, SSM_GROUP
    K = L * P
    rowf = lambda v: v.reshape(2, G, 1, N)
    colf = lambda v: v.reshape(2, G, N, 1)
    ldt = jnp.broadcast_to(log_dt[:, :, None], (2, G, N))
    tr = lambda v: jnp.swapaxes(v, 2, 3)
    ct = lambda v: jnp.tile(tr(v), (1, 1, 1, LANES // P))
    spec = lambda shape: pl.BlockSpec((2, 1) + shape, lambda g: (0, g, 0, 0))
    t_sum, p_cat, q_cat, al = pl.pallas_call(
        _ssm_ops_kernel, grid=(G,),
        in_specs=[spec((1, N))] * 3 + [spec((N, 1))] * 3 + [spec((P, N))] * 2 + [spec((N, LANES))] * 2,
        out_specs=[pl.BlockSpec((1, K, K), lambda g: (g, 0, 0)), pl.BlockSpec((1, K, 4 * N), lambda g: (g, 0, 0)),
                   pl.BlockSpec((1, 4 * N, K), lambda g: (g, 0, 0)), pl.BlockSpec((1, 2, 2 * N), lambda g: (g, 0, 0))],
        out_shape=[jax.ShapeDtypeStruct((G, K, K), BF16), jax.ShapeDtypeStruct((G, K, 4 * N), BF16),
                   jax.ShapeDtypeStruct((G, 4 * N, K), BF16), jax.ShapeDtypeStruct((G, 2, 2 * N), F32)],
        compiler_params=_params(("parallel",)), name="ssm_ops",
    )(rowf(lam_re), rowf(lam_im), rowf(ldt), colf(lam_re), colf(lam_im), colf(ldt),
      tr(b_re), tr(b_im), ct(c_re), ct(c_im))
    return t_sum, p_cat, q_cat, al[:, 0, :].reshape(1, G * 2 * N), al[:, 1, :].reshape(1, G * 2 * N)


def _ssm_state_in_kernel(u_ref, p_ref, xre_ref, xim_ref):
    x = _dot(u_ref[0], p_ref[0])
    xre_ref[...] = x[:, 0:LANES]
    xim_ref[...] = x[:, LANES:2 * LANES]


def _ssm_state_in(u_t, p_cat):
    G, M, K = u_t.shape
    tm = min(M, 256)
    return pl.pallas_call(
        _ssm_state_in_kernel, grid=(G, M // tm),
        in_specs=[pl.BlockSpec((1, tm, K), lambda g, m: (g, m, 0)),
                  pl.BlockSpec((1, K, 2 * LANES), lambda g, m: (g, 0, 0))],
        out_specs=[pl.BlockSpec((tm, LANES), lambda g, m: (m, g))] * 2,
        out_shape=[jax.ShapeDtypeStruct((M, G * LANES), F32)] * 2,
        compiler_params=_params(("parallel", "parallel")), name="ssm_state_in",
    )(u_t, p_cat)


def _ssm_scan_kernel(xre_ref, xim_ref, are_ref, aim_ref, hfre_ref, hfim_ref, hbre_ref, hbim_ref):
    nc = xre_ref.shape[1]
    width = xre_ref.shape[2]
    ar = are_ref[...]
    ai = aim_ref[...]
    lane = lax.broadcasted_iota(jnp.int32, (1, width), 1)
    is_fwd = (lane % LANES) < SSM_STATE

    def step(i, carry):
        hr, hi = carry
        ib = nc - 1 - i
        hfre_ref[0, pl.ds(i, 1), :] = hr
        hfim_ref[0, pl.ds(i, 1), :] = hi
        hbre_ref[0, pl.ds(ib, 1), :] = hr
        hbim_ref[0, pl.ds(ib, 1), :] = hi
        xr = jnp.where(is_fwd, xre_ref[0, pl.ds(i, 1), :], xre_ref[0, pl.ds(ib, 1), :])
        xi = jnp.where(is_fwd, xim_ref[0, pl.ds(i, 1), :], xim_ref[0, pl.ds(ib, 1), :])
        return ar * hr - ai * hi + xr, ar * hi + ai * hr + xi

    zero = jnp.zeros((1, width), F32)
    lax.fori_loop(0, nc, step, (zero, zero))


def _ssm_scan(xre, xim, al_re, al_im):
    B, nc, width = xre.shape
    blk = pl.BlockSpec((1, nc, width), lambda b: (b, 0, 0))
    vec = pl.BlockSpec((1, width), lambda b: (0, 0))
    return pl.pallas_call(
        _ssm_scan_kernel, grid=(B,),
        in_specs=[blk, blk, vec, vec], out_specs=[blk] * 4,
        out_shape=[jax.ShapeDtypeStruct((B, nc, width), F32)] * 4,
        compiler_params=_params(("parallel",)), name="ssm_scan",
    )(xre, xim, al_re, al_im)


def _ssm_out_kernel(u_ref, t_ref, hfre_ref, hfim_ref, hbre_ref, hbim_ref, q_ref, y_ref):
    tm = u_ref.shape[1]
    lane = lax.broadcasted_iota(jnp.int32, (tm, LANES), 1)
    is_fwd = lane < SSM_STATE
    h_re = jnp.where(is_fwd, hfre_ref[...], hbre_ref[...])
    h_im = jnp.where(is_fwd, hfim_ref[...], hbim_ref[...])
    h_cat = jnp.concatenate([h_re, h_im], axis=1).astype(BF16)
    y_ref[0] = _dot(u_ref[0], t_ref[0]) + _dot(h_cat, q_ref[0])


def _ssm_out(u_t, t_sum, hs, q_cat):
    G, M, K = u_t.shape
    tm = min(M, 256)
    hblk = pl.BlockSpec((tm, LANES), lambda g, m: (m, g))
    return pl.pallas_call(
        _ssm_out_kernel, grid=(G, M // tm),
        in_specs=[pl.BlockSpec((1, tm, K), lambda g, m: (g, m, 0)),
                  pl.BlockSpec((1, K, K), lambda g, m: (g, 0, 0)),
                  hblk, hblk, hblk, hblk,
                  pl.BlockSpec((1, 2 * LANES, K), lambda g, m: (g, 0, 0))],
        out_specs=pl.BlockSpec((1, tm, K), lambda g, m: (g, m, 0)),
        out_shape=jax.ShapeDtypeStruct((G, M, K), F32),
        compiler_params=_params(("parallel", "parallel")), name="ssm_out",
    )(u_t, t_sum, *hs, q_cat)


def _s5_scan(u, ops):
    t_sum, p_cat, q_cat, al_re, al_im = ops
    B, S, _ = u.shape
    L, G, P = SSM_CHUNK, SSM_GROUPS, SSM_GROUP
    nc = S // L
    u_t = u.reshape(B, nc, L, G, P).transpose(3, 0, 1, 2, 4).reshape(G, B * nc, L * P).astype(BF16)
    xre, xim = _ssm_state_in(u_t, p_cat)
    width = G * LANES
    hs = _ssm_scan(xre.reshape(B, nc, width), xim.reshape(B, nc, width), al_re, al_im)
    hfre, hfim, hbre, hbim = (h.reshape(B * nc, width) for h in hs)
    y = _ssm_out(u_t, t_sum, (hfre, hfim, hbre, hbim), q_cat)
    return y.reshape(G, B, nc, L, P).transpose(1, 2, 3, 0, 4).reshape(B, S, G * P)


def _na_bias_kernel(rpb_ref, out_ref):
    h = pl.program_id(0)
    n_dr, n_dc = 2 * NA_WIN_R - 1, 2 * NA_WIN_C - 1
    qc = lax.broadcasted_iota(jnp.int32, (GRID_W, GRID_W), 0)
    kc = lax.broadcasted_iota(jnp.int32, (GRID_W, GRID_W), 1)
    dc = jnp.clip(kc - qc + (NA_WIN_C - 1), 0, n_dc - 1)
    cstart = jnp.clip(qc - NA_WIN_C // 2, 0, GRID_W - NA_WIN_C)
    col_ok = (kc >= cstart) & (kc < cstart + NA_WIN_C)
    base = []
    for dr in range(n_dr):
        acc = jnp.zeros((GRID_W, GRID_W), F32)
        for d in range(n_dc):
            acc = jnp.where(dc == d, rpb_ref[(h * n_dr + dr) * n_dc + d], acc)
        base.append(jnp.where(col_ok, acc, NEG))
    for off in range(NA_WIN_R):
        out_ref[0, off] = jnp.concatenate([base[w - off + NA_WIN_R - 1] for w in range(NA_WIN_R)], axis=1)


def _na_bias_table(rpb):
    return pl.pallas_call(
        _na_bias_kernel, grid=(HEADS,),
        in_specs=[pl.BlockSpec(memory_space=pltpu.SMEM)],
        out_specs=pl.BlockSpec((1, NA_WIN_R, GRID_W, NA_WIN_R * GRID_W), lambda h: (h, 0, 0, 0)),
        out_shape=jax.ShapeDtypeStruct((HEADS, NA_WIN_R, GRID_W, NA_WIN_R * GRID_W), F32),
        compiler_params=_params(("parallel",)), name="na_bias",
    )(rpb.astype(F32).reshape(-1))


def _na_kernel(q_ref, kp_ref, kc_ref, kn_ref, vp_ref, vc_ref, vn_ref, bias_ref, o_ref, kbuf, vbuf, *, rows):
    rb = pl.program_id(1)
    Wg = GRID_W
    C = HEAD_DIM
    halo = NA_HALO_ROWS * Wg
    cur = NA_ROWS_PER_STEP * Wg
    win = NA_WIN_R * Wg
    kbuf[0:halo, :] = kp_ref[0]
    kbuf[halo:halo + cur, :] = kc_ref[0]
    kbuf[halo + cur:2 * halo + cur, :] = kn_ref[0]
    vbuf[0:halo, :] = vp_ref[0]
    vbuf[halo:halo + cur, :] = vc_ref[0]
    vbuf[halo + cur:2 * halo + cur, :] = vn_ref[0]
    r0 = rb * NA_ROWS_PER_STEP
    for j in range(NA_ROWS_PER_STEP):
        r = r0 + j
        rs = jnp.clip(r - NA_WIN_R // 2, 0, rows - NA_WIN_R)
        off = r - rs
        start = pl.multiple_of((rs - r0 + NA_HALO_ROWS) * Wg, Wg)
        for h in range(HEADS):
            sl = slice(h * C, (h + 1) * C)
            qh = q_ref[0, j * Wg:(j + 1) * Wg, sl]
            kw = kbuf[pl.ds(start, win), sl]
            vw = vbuf[pl.ds(start, win), sl]
            s = _dot_nt(qh, kw) + bias_ref[h, off]
            m = jnp.max(s, axis=-1, keepdims=True)
            p = jnp.exp(s - m)
            l = jnp.sum(p, axis=-1, keepdims=True)
            o_ref[0, j * Wg:(j + 1) * Wg, sl] = _dot(p.astype(BF16), vw) / l


def _neighbourhood_attention(q, k, v, bias):
    B, S, W = q.shape
    rows = S // GRID_W
    cur = NA_ROWS_PER_STEP * GRID_W
    halo = NA_HALO_ROWS * GRID_W
    nrb = S // cur
    per = cur // halo
    nh = S // halo
    blk = pl.BlockSpec((1, cur, W), lambda b, i: (b, i, 0))
    prev = pl.BlockSpec((1, halo, W), lambda b, i: (b, jnp.maximum(i * per - 1, 0), 0))
    nxt = pl.BlockSpec((1, halo, W), lambda b, i: (b, jnp.minimum((i + 1) * per, nh - 1), 0))
    bspec = pl.BlockSpec(bias.shape, lambda b, i: (0, 0, 0, 0))
    return pl.pallas_call(
        functools.partial(_na_kernel, rows=rows), grid=(B, nrb),
        in_specs=[blk, prev, blk, nxt, prev, blk, nxt, bspec],
        out_specs=blk, out_shape=jax.ShapeDtypeStruct((B, S, W), F32),
        scratch_shapes=[pltpu.VMEM((cur + 2 * halo, W), BF16), pltpu.VMEM((cur + 2 * halo, W), BF16)],
        compiler_params=_params(("parallel", "parallel")), name="neighbourhood_attn",
    )(q, k, k, k, v, v, v, bias)


def _post_kernel(x_ref, of_ref, ob_ref, gate_ref, yb_ref, ys_ref, u_ref, yd_ref,
                 dnw_ref, bd_ref, dskip_ref, gluw_ref, glub_ref, wout_ref, ln2_ref, w1_ref, w2_ref, fnw_ref,
                 out_ref, *, final):
    x = x_ref[...]
    o = of_ref[...] + ob_ref[...]
    ms = _head_sums(o * o, bd_ref[...]) * (1.0 / HEAD_DIM)
    g = gate_ref[...]
    y_a = o * lax.rsqrt(ms + EPS) * dnw_ref[...] * (g * _sigmoid(g))
    y = ys_ref[...] + dskip_ref[...] * u_ref[...]
    z = 0.5 * y * (1.0 + jnp.tanh(math.sqrt(2.0 / math.pi) * (y + 0.044715 * (y * y * y))))
    y_c = z * _sigmoid(_dot(z.astype(BF16), gluw_ref[...]) + glub_ref[...])
    mix = jnp.concatenate([y_a, yb_ref[...], y_c, yd_ref[...]], axis=1).astype(BF16)
    x1 = x + _dot(mix, wout_ref[...])
    hb = _rms(x1, ln2_ref[...]).astype(BF16)
    ffc = D_MODEL
    acc = x1
    for c in range(D_FF // ffc):
        a = jnp.maximum(_dot(hb, w1_ref[:, c * ffc:(c + 1) * ffc]), 0.0)
        acc = acc + _dot((a * a).astype(BF16), w2_ref[c * ffc:(c + 1) * ffc, :])
    if final:
        acc = _rms(acc, fnw_ref[...])
    out_ref[...] = acc


def _post(x, o_f, o_b, gate, y_b, y_s, u, y_d, wts, final):
    T = x.shape[0]
    tm = min(TOKEN_TILE, T)
    W = WIDTH
    row = lambda n: pl.BlockSpec((tm, n), lambda i: (i, 0))
    const = lambda a: pl.BlockSpec(a.shape, lambda i: (0, 0), pipeline_mode=pl.Buffered(1))
    return pl.pallas_call(
        functools.partial(_post_kernel, final=final), grid=(T // tm,),
        in_specs=[row(D_MODEL)] + [row(W)] * 7 + [const(a) for a in wts],
        out_specs=row(D_MODEL), out_shape=jax.ShapeDtypeStruct((T, D_MODEL), F32),
        compiler_params=_params(("parallel",)), name="post",
    )(x, o_f, o_b, gate, y_b, y_s, u, y_d, *wts)


def _rope_tables(seq):
    inv_freq = ROPE_THETA ** (-jnp.arange(0, HEAD_DIM, 2, dtype=F32) / HEAD_DIM)
    ang = jnp.arange(seq, dtype=F32)[:, None] * inv_freq[None, :]
    cos, sin = jnp.cos(ang), jnp.sin(ang)
    cos_t = jnp.tile(jnp.concatenate([cos, cos], axis=1), (1, HEADS))
    sin_t = jnp.tile(jnp.concatenate([-sin, sin], axis=1), (1, HEADS))
    return cos_t, sin_t


def _layer_weights(i, p):
    W = WIDTH
    w = p['w_in'][i]
    o_gate, o_a, o_b, o_dil, o_ssm, o_na = 3 * W, 4 * W, 4 * W + 2 * HEADS, 4 * W + 4 * HEADS, 7 * W + 4 * HEADS, 8 * W + 4 * HEADS
    pad = jnp.zeros((D_MODEL, LANES - 4 * HEADS), F32)
    w_in = jnp.concatenate([w[:, 0:o_a], w[:, o_dil:o_na + 3 * W], w[:, o_a:o_dil], pad], axis=1).astype(BF16)
    assert w_in.shape[1] == IN_COLS_PADDED
    lane_pad = lambda v: jnp.concatenate([v.reshape(1, -1).astype(F32), jnp.zeros((1, LANES - v.size), F32)], axis=1)
    idx = np.arange(W)
    bd = jnp.asarray((idx[:, None] // HEAD_DIM) == (idx[None, :] // HEAD_DIM), BF16)
    row = lambda v: v.reshape(1, -1).astype(F32)
    return dict(
        ln1=row(p['ln1_w'][i]), w_in=w_in,
        conv_w=p['dn_conv_w'][i].astype(F32), alog=lane_pad(p['dn_a_log'][i]), dtb=lane_pad(p['dn_dt_bias'][i]),
        bd=bd,
        ssm=_ssm_operators(*(p[n][i].astype(F32) for n in
                             ('ssm_lam_re', 'ssm_lam_im', 'ssm_log_dt', 'ssm_b_re', 'ssm_b_im', 'ssm_c_re', 'ssm_c_im'))),
        na_bias=_na_bias_table(p['na_rpb'][i]),
        post=(row(jnp.tile(p['dn_norm_w'][i], HEADS)), bd, row(p['ssm_d'][i]), p['ssm_glu_w'][i].astype(BF16),
              row(p['ssm_glu_b'][i]), p['w_out'][i].astype(BF16), row(p['ln2_w'][i]),
              p['w_ff1'][i].astype(BF16), p['w_ff2'][i].astype(BF16), row(p['final_norm_w'])),
    )


def _trunk(x, layers, dil_bias):
    B, S, D = x.shape
    T = B * S
    cos_t, sin_t = _rope_tables(S)
    xt = x.reshape(T, D)
    seq3 = lambda a: a.reshape(B, S, a.shape[-1])
    flat = lambda a: a.reshape(T, a.shape[-1])
    for i, lw in enumerate(layers):
        z_dn, z_gate, z_ab, dq, dk, dv, z_ssm, nq, nk, nv = _in_proj(xt, lw['ln1'], lw['w_in'], cos_t, sin_t, S)
        q, k, v, gb = _dn_prep(seq3(z_dn), seq3(z_ab), lw['conv_w'], lw['alog'], lw['dtb'], lw['bd'])
        o_f = _dn_scan(q, k, v, gb, False)
        o_b = _dn_scan(q, k, v, gb, True)
        y_b = _dilated_attention(seq3(dq), seq3(dk), seq3(dv), dil_bias)
        y_s = _s5_scan(seq3(z_ssm), lw['ssm'])
        y_d = _neighbourhood_attention(seq3(nq), seq3(nk), seq3(nv), lw['na_bias'])
        xt = _post(xt, flat(o_f), flat(o_b), z_gate, flat(y_b), flat(y_s), z_ssm, flat(y_d),
                   lw['post'], final=(i == len(layers) - 1))
    return xt.reshape(B, S, D)


def kernel(x_prompt, x_sample, ln1_w, w_in, dn_conv_w, dn_a_log, dn_dt_bias, dn_norm_w, ssm_lam_re, ssm_lam_im,
           ssm_log_dt, ssm_b_re, ssm_b_im, ssm_c_re, ssm_c_im, ssm_d, ssm_glu_w, ssm_glu_b, na_rpb, w_out,
           ln2_w, w_ff1, w_ff2, final_norm_w):
    p = dict(ln1_w=ln1_w, w_in=w_in, dn_conv_w=dn_conv_w, dn_a_log=dn_a_log, dn_dt_bias=dn_dt_bias,
             dn_norm_w=dn_norm_w, ssm_lam_re=ssm_lam_re, ssm_lam_im=ssm_lam_im, ssm_log_dt=ssm_log_dt,
             ssm_b_re=ssm_b_re, ssm_b_im=ssm_b_im, ssm_c_re=ssm_c_re, ssm_c_im=ssm_c_im, ssm_d=ssm_d,
             ssm_glu_w=ssm_glu_w, ssm_glu_b=ssm_glu_b, na_rpb=na_rpb, w_out=w_out, ln2_w=ln2_w,
             w_ff1=w_ff1, w_ff2=w_ff2, final_norm_w=final_norm_w)
    layers = [_layer_weights(i, p) for i in range(DEPTH)]
    dil_bias = jnp.asarray(_dil_bias_tiles())
    return (_trunk(x_prompt, layers, dil_bias), _trunk(x_sample, layers, dil_bias))
```

```python
import functools
import math

import numpy as np
import jax
import jax.numpy as jnp
from jax import lax
from jax.experimental import pallas as pl
from jax.experimental.pallas import tpu as pltpu

F32 = jnp.float32
BF16 = jnp.bfloat16

D_MODEL = 1024
DEPTH = 2
HEAD_DIM = 64
HEADS = 4
WIDTH = HEADS * HEAD_DIM
DN_CHUNK = 64
CONV_WIDTH = 5
DIL_PATTERNS = ((128, 1), (512, 4), (2048, 16))
SSM_GROUP = 16
SSM_GROUPS = 16
SSM_STATE = 64
SSM_CHUNK = 64
GRID_W = 64
NA_WIN_R = 8
NA_WIN_C = 16
D_FF = 4 * D_MODEL
ROPE_THETA = 10000.0
EPS = 1e-6
NEG = -1e30

LANES = 128
TOKEN_TILE = 512
DIL_TILE = 1024
DIL_BAND = 64
NA_ROWS_PER_STEP = 8
NA_HALO_ROWS = 4
IN_COLS_PADDED = 23 * LANES
VMEM_LIMIT = 56 * 2**20


def _params(sem):
    return pltpu.CompilerParams(dimension_semantics=sem, vmem_limit_bytes=VMEM_LIMIT)


def _dot(a, b):
    return jnp.dot(a, b, preferred_element_type=F32)


def _dot_nt(a, b):
    return lax.dot_general(a, b, (((1,), (1,)), ((), ())), preferred_element_type=F32)


def _bmm(a, b):
    return lax.dot_general(a, b, (((2,), (1,)), ((0,), (0,))), preferred_element_type=F32)


def _bmm_nt(a, b):
    return lax.dot_general(a, b, (((2,), (2,)), ((0,), (0,))), preferred_element_type=F32)


def _split2(x):
    hi = x.astype(BF16)
    lo = (x - hi.astype(F32)).astype(BF16)
    return hi, lo


def _split3(x):
    hi = x.astype(BF16)
    r = x - hi.astype(F32)
    mid = r.astype(BF16)
    lo = (r - mid.astype(F32)).astype(BF16)
    return hi, mid, lo


def _exact_lhs_dot(m, x):
    hi, mid, lo = _split3(x)
    return _dot(m, hi) + _dot(m, mid) + _dot(m, lo)


def _head_sums(sq, bd):
    hi, lo = _split2(sq)
    return _dot(hi, bd) + _dot(lo, bd)


def _sigmoid(x):
    return 1.0 / (1.0 + jnp.exp(-x))


def _rms(x, w):
    ms = jnp.mean(x * x, axis=-1, keepdims=True)
    return x * lax.rsqrt(ms + EPS) * w


def _in_proj_kernel(x_ref, lnw_ref, w_ref, cos_ref, sin_ref,
                    dn_ref, gate_ref, ab_ref, dq_ref, dk_ref, dv_ref, ssm_ref, nq_ref, nk_ref, nv_ref):
    h = _rms(x_ref[...], lnw_ref[...]).astype(BF16)
    z = _dot(h, w_ref[...])
    tm = z.shape[0]
    W = WIDTH
    dn_ref[...] = z[:, 0:3 * W]
    gate_ref[...] = z[:, 3 * W:4 * W]
    cos = cos_ref[...]
    sin = sin_ref[...]
    lane = lax.broadcasted_iota(jnp.int32, (tm, W), 1)
    first_half = (lane % HEAD_DIM) < (HEAD_DIM // 2)

    def rope(t):
        rot = jnp.where(first_half, pltpu.roll(t, W - HEAD_DIM // 2, 1), pltpu.roll(t, HEAD_DIM // 2, 1))
        return t * cos + rot * sin

    scale = HEAD_DIM ** -0.5
    for ref, t in ((dq_ref, rope(z[:, 4 * W:5 * W]) * scale), (dk_ref, rope(z[:, 5 * W:6 * W])),
                   (dv_ref, z[:, 6 * W:7 * W])):
        ref[0] = t[:, 0:LANES]
        ref[1] = t[:, LANES:2 * LANES]
    ssm_ref[...] = z[:, 7 * W:8 * W]
    nq_ref[...] = (z[:, 8 * W:9 * W] * scale).astype(BF16)
    nk_ref[...] = z[:, 9 * W:10 * W].astype(BF16)
    nv_ref[...] = z[:, 10 * W:11 * W].astype(BF16)
    ab_ref[...] = z[:, 11 * W:11 * W + LANES]


def _in_proj(x, lnw, w, cos_t, sin_t, seq):
    T = x.shape[0]
    tm = min(TOKEN_TILE, seq)
    per_seq = seq // tm
    W = WIDTH
    row = lambda n: pl.BlockSpec((tm, n), lambda i: (i, 0))
    const = lambda a: pl.BlockSpec(a.shape, lambda i: (0, 0))
    tab = pl.BlockSpec((tm, W), lambda i: (i % per_seq, 0))
    pair = pl.BlockSpec((2, tm, LANES), lambda i: (0, i, 0))
    pair_shape = jax.ShapeDtypeStruct((2, T, LANES), F32)
    out_shapes = [
        jax.ShapeDtypeStruct((T, 3 * W), F32), jax.ShapeDtypeStruct((T, W), F32),
        jax.ShapeDtypeStruct((T, LANES), F32),
        pair_shape, pair_shape, pair_shape,
        jax.ShapeDtypeStruct((T, W), F32),
        jax.ShapeDtypeStruct((T, W), BF16), jax.ShapeDtypeStruct((T, W), BF16), jax.ShapeDtypeStruct((T, W), BF16),
    ]
    out_specs = [row(3 * W), row(W), row(LANES), pair, pair, pair, row(W), row(W), row(W), row(W)]
    return pl.pallas_call(
        _in_proj_kernel, grid=(T // tm,),
        in_specs=[row(D_MODEL), const(lnw), const(w), tab, tab],
        out_specs=out_specs, out_shape=out_shapes,
        compiler_params=_params(("parallel",)), name="in_proj",
    )(x, lnw, w, cos_t, sin_t)


def _dn_prep_kernel(zc_ref, zp_ref, zn_ref, ab_ref, cw_ref, alog_ref, dtb_ref, bd_ref,
                    q_ref, k_ref, v_ref, gb_ref, buf):
    i = pl.program_id(1)
    n = pl.num_programs(1)
    tb = zc_ref.shape[1]
    W = WIDTH
    half = CONV_WIDTH // 2
    buf[0:8, :] = jnp.where(i > 0, zp_ref[0], 0.0)
    buf[8:8 + tb, :] = zc_ref[0]
    buf[8 + tb:16 + tb, :] = jnp.where(i < n - 1, zn_ref[0], 0.0)
    cw = cw_ref[...]
    acc = buf[8 - half:8 - half + tb, :] * cw[0:1, :]
    for j in range(1, CONV_WIDTH):
        acc = acc + buf[8 - half + j:8 - half + j + tb, :] * cw[j:j + 1, :]
    y = acc * _sigmoid(acc)
    bd = bd_ref[...]
    q = y[:, 0:W]
    k = y[:, W:2 * W]
    q_ref[0] = q * lax.rsqrt(_head_sums(q * q, bd) + EPS) * (HEAD_DIM ** -0.5)
    k_ref[0] = k * lax.rsqrt(_head_sums(k * k, bd) + EPS)
    v_ref[0] = y[:, 2 * W:3 * W]
    ab = ab_ref[0]
    t = ab + dtb_ref[...]
    softplus = jnp.maximum(t, 0.0) + jnp.log(1.0 + jnp.exp(-jnp.abs(t)))
    g = -jnp.exp(alog_ref[...]) * softplus
    lane = lax.broadcasted_iota(jnp.int32, ab.shape, 1)
    gb_ref[0] = jnp.where(lane < 2 * HEADS, g, _sigmoid(ab))


def _dn_prep(z_dn, z_ab, conv_w, alog, dtb, bd):
    B, S, C = z_dn.shape
    tb = min(TOKEN_TILE, S)
    nb = S // tb
    r8 = tb // 8
    W = WIDTH
    blk = lambda n: pl.BlockSpec((1, tb, n), lambda b, i: (b, i, 0))
    const = lambda a: pl.BlockSpec(a.shape, lambda b, i: (0, 0))
    prev = pl.BlockSpec((1, 8, C), lambda b, i: (b, jnp.maximum(i * r8 - 1, 0), 0))
    nxt = pl.BlockSpec((1, 8, C), lambda b, i: (b, jnp.minimum((i + 1) * r8, S // 8 - 1), 0))
    return pl.pallas_call(
        _dn_prep_kernel, grid=(B, nb),
        in_specs=[blk(C), prev, nxt, blk(LANES), const(conv_w), const(alog), const(dtb), const(bd)],
        out_specs=[blk(W), blk(W), blk(W), blk(LANES)],
        out_shape=[jax.ShapeDtypeStruct((B, S, W), F32)] * 3 + [jax.ShapeDtypeStruct((B, S, LANES), F32)],
        scratch_shapes=[pltpu.VMEM((tb + 16, C), F32)],
        compiler_params=_params(("parallel", "parallel")), name="dn_prep",
    )(z_dn, z_dn, z_dn, z_ab, conv_w, alog, dtb, bd)


def _dn_scan_kernel(q_ref, k_ref, v_ref, gb_ref, o_ref,
                    s_scr, u_scr, w_scr, qg_scr, at_scr, kdt_scr, *, rev, nc):
    C = DN_CHUNK
    tb = nc * C
    H = HEADS
    assert LANES == 2 * C

    @pl.when(pl.program_id(1) == 0)
    def _():
        s_scr[...] = jnp.zeros_like(s_scr)

    gb = gb_ref[0]
    r = lax.broadcasted_iota(jnp.int32, (tb, tb), 0)
    c = lax.broadcasted_iota(jnp.int32, (tb, tb), 1)
    same = (r // C) == (c // C)
    cum = same & ((c >= r) if rev else (c <= r))
    gam = _exact_lhs_dot(jnp.where(cum, 1.0, 0.0).astype(BF16), gb)
    gtot = _exact_lhs_dot(jnp.where(same, 1.0, 0.0).astype(BF16), gb)
    ri = lax.broadcasted_iota(jnp.int32, (LANES, LANES), 0)
    ci = lax.broadcasted_iota(jnp.int32, (LANES, LANES), 1)
    eye128 = jnp.where(ri == ci, 1.0, 0.0).astype(BF16)
    same_head = (ri < C) == (ci < C)
    g_hi, g_mid, g_lo = _split3(gam)
    gam_t = _dot_nt(eye128, g_hi) + _dot_nt(eye128, g_mid) + _dot_nt(eye128, g_lo)

    lane = lax.broadcasted_iota(jnp.int32, (tb, LANES), 1)
    first = lane < C
    ii = lax.broadcasted_iota(jnp.int32, (tb, LANES), 0) % C
    jj = lane % C
    incl = (jj >= ii) if rev else (jj <= ii)
    strict = (jj > ii) if rev else (jj < ii)
    eye_f = jnp.where(ii == jj, 1.0, 0.0)
    lane_c = lax.broadcasted_iota(jnp.int32, (C, LANES), 1)

    def block_diag(x):
        ln = lax.broadcasted_iota(jnp.int32, x.shape, x.ndim - 1)
        zero = jnp.zeros_like(x)
        return jnp.concatenate([jnp.where(ln < C, x, zero), jnp.where(ln >= C, x, zero)], axis=x.ndim - 2)

    chunks = lambda x: x.astype(BF16).reshape(nc, C, x.shape[-1])
    for pr in range(H // 2):
        ch0 = (H if rev else 0) + 2 * pr
        tile = slice(pr * LANES, (pr + 1) * LANES)
        pick = lambda a, col: jnp.where(first, a[:, col:col + 1], a[:, col + 1:col + 2])
        gam_c = pick(gam, ch0)
        gtot_c = pick(gtot, ch0)
        beta = pick(gb, 2 * H + ch0)
        g_rows = jnp.concatenate(
            [jnp.concatenate([jnp.broadcast_to(gam_t[ch0:ch0 + 1, cc * C:(cc + 1) * C], (C, C)),
                              jnp.broadcast_to(gam_t[ch0 + 1:ch0 + 2, cc * C:(cc + 1) * C], (C, C))], axis=1)
             for cc in range(nc)], axis=0)
        dec = jnp.exp(jnp.where(incl, gam_c - g_rows, NEG))
        qp = q_ref[0, :, tile]
        kp = k_ref[0, :, tile]
        vp = v_ref[0, :, tile]
        kb = kp * beta
        k_bd = block_diag(chunks(kp))
        a_raw = _bmm_nt(chunks(kb), k_bd).reshape(tb, LANES)
        qk = _bmm_nt(chunks(qp), k_bd).reshape(tb, LANES)
        lm = jnp.where(strict, a_raw * dec, 0.0)
        attn = qk * dec
        t_inv = eye_f - jnp.where((ii // 2) == (jj // 2), lm, 0.0)
        n = 2
        while n < C:
            off = ((ii // (2 * n)) == (jj // (2 * n))) & ((ii // n) != (jj // n))
            t3 = chunks(t_inv)
            bt = _bmm(chunks(jnp.where(off, lm, 0.0)), block_diag(t3))
            t_inv = t_inv - _bmm(t3, block_diag(bt.astype(BF16))).reshape(tb, LANES)
            n *= 2
        eg = jnp.exp(gam_c)
        rhs = jnp.concatenate([block_diag(chunks(vp * beta)), block_diag(chunks(kb * eg))], axis=2)
        sol = _bmm(chunks(t_inv), rhs).reshape(tb, 2 * LANES)
        u_scr[pr] = sol[:, 0:LANES]
        w_scr[pr] = sol[:, LANES:2 * LANES].astype(BF16)
        qg_scr[pr] = (qp * eg).astype(BF16)
        at_scr[pr] = attn.astype(BF16)
        kd = kp * jnp.exp(gtot_c - gam_c)
        kdt_scr[pr] = _dot_nt(eye128, kd.astype(BF16)).astype(BF16)

    order = range(nc - 1, -1, -1) if rev else range(nc)
    for cc in order:
        lo, hi = cc * C, (cc + 1) * C
        for pr in range(H // 2):
            ch0 = (H if rev else 0) + 2 * pr
            s = s_scr[pr]
            sb = s.astype(BF16)
            v_new = u_scr[pr, lo:hi, :] - _dot(w_scr[pr, lo:hi, :], sb)
            vb = v_new.astype(BF16)
            o = _dot(qg_scr[pr, lo:hi, :], sb) + _dot(at_scr[pr, lo:hi, :], block_diag(vb))
            o_ref[0, lo:hi, pr * LANES:(pr + 1) * LANES] = o
            gl = jnp.exp(jnp.where(lane_c[0:1] < C, gtot[lo:lo + 1, ch0:ch0 + 1], gtot[lo:lo + 1, ch0 + 1:ch0 + 2]))
            upd = _dot(kdt_scr[pr, :, lo:hi], vb)
            s_scr[pr] = s * gl + jnp.where(same_head, upd, 0.0)


def _dn_scan(q, k, v, gb, rev):
    B, S, W = q.shape
    tb = min(TOKEN_TILE, S)
    nb = S // tb
    nc = tb // DN_CHUNK
    C = DN_CHUNK
    if rev:
        imap = lambda b, i: (b, nb - 1 - i, 0)
    else:
        imap = lambda b, i: (b, i, 0)
    blk = lambda n: pl.BlockSpec((1, tb, n), imap)
    return pl.pallas_call(
        functools.partial(_dn_scan_kernel, rev=rev, nc=nc), grid=(B, nb),
        in_specs=[blk(W), blk(W), blk(W), blk(LANES)],
        out_specs=blk(W), out_shape=jax.ShapeDtypeStruct((B, S, W), F32),
        scratch_shapes=[pltpu.VMEM((HEADS // 2, LANES, LANES), F32), pltpu.VMEM((HEADS // 2, tb, LANES), F32),
                        pltpu.VMEM((HEADS // 2, tb, LANES), BF16), pltpu.VMEM((HEADS // 2, tb, LANES), BF16),
                        pltpu.VMEM((HEADS // 2, tb, LANES), BF16), pltpu.VMEM((HEADS // 2, LANES, tb), BF16)],
        compiler_params=_params(("parallel", "arbitrary")), name="dn_scan_bwd" if rev else "dn_scan_fwd",
    )(q, k, v, gb)


def _dil_kernel(q_ref, kp_ref, kc_ref, kn_ref, vp_ref, vc_ref, vn_ref, o_ref,
                kwin, vwin, m_st, l_st, acc_st, *, seq):
    i = pl.program_id(2)
    tq = DIL_TILE
    band = DIL_BAND
    C = HEAD_DIM
    assert LANES == 2 * C
    kwin[0:tq, :] = kp_ref[0, 0]
    kwin[tq:2 * tq, :] = kc_ref[0, 0]
    kwin[2 * tq:3 * tq, :] = kn_ref[0, 0]
    vwin[0:tq, :] = vp_ref[0, 0]
    vwin[tq:2 * tq, :] = vc_ref[0, 0]
    vwin[2 * tq:3 * tq, :] = vn_ref[0, 0]

    def rows(start, size, stride):
        return pl.ds(start, size) if stride == 1 else pl.ds(start, size, stride=stride)

    for pi, (window, dil) in enumerate(DIL_PATTERNS):
        assert window // (2 * dil) == band and band * dil <= tq
        qb = min(2 * band, tq // dil)
        nk = qb + 2 * band
        qa = lax.broadcasted_iota(jnp.int32, (qb, nk), 0)
        kj = lax.broadcasted_iota(jnp.int32, (qb, nk), 1)
        band_bias = jnp.where((kj >= qa) & (kj <= qa + 2 * band), 0.0, NEG)
        kcol = lax.broadcasted_iota(jnp.int32, (1, nk), 1)
        vlane = lax.broadcasted_iota(jnp.int32, (nk, LANES), 1)
        qlane = lax.broadcasted_iota(jnp.int32, (qb, LANES), 1)
        one = jnp.ones((nk, LANES), BF16)
        units = [(c, u) for c in range(dil) for u in range(tq // (dil * qb))]
        group = 4
        first = qlane < C
        for g0 in range(0, len(units), group):
            scores, values = [], []
            for c, u in units[g0:g0 + group]:
                q0 = c + dil * u * qb
                k0 = tq + q0 - dil * band
                pos = i * tq + (q0 - dil * band) + dil * kcol
                bias = band_bias + jnp.where((pos >= 0) & (pos < seq), 0.0, NEG)
                qr = q_ref[0, 0, rows(q0, qb, dil), :].astype(BF16)
                kr = kwin[rows(k0, nk, dil), :].astype(BF16)
                values.append(vwin[rows(k0, nk, dil), :].astype(BF16))
                for hh in range(LANES // C):
                    sl = slice(hh * C, (hh + 1) * C)
                    scores.append(_dot_nt(qr[:, sl], kr[:, sl]) + bias)
            s = jnp.concatenate(scores, axis=0)
            m = jnp.max(s, axis=-1, keepdims=True)
            pb = jnp.exp(s - m).astype(BF16)
            for g, (c, u) in enumerate(units[g0:g0 + group]):
                b0 = slice(2 * g * qb, (2 * g + 1) * qb)
                b1 = slice((2 * g + 1) * qb, (2 * g + 2) * qb)
                pvl0 = _dot(pb[b0], jnp.where(vlane < C, values[g], one))
                pvl1 = _dot(pb[b1], jnp.where(vlane >= C, values[g], one))
                sel = rows(c + dil * u * qb, qb, dil)
                m_st[pi, sel, :] = jnp.where(first, m[b0], m[b1])
                l_st[pi, sel, :] = pltpu.roll(jnp.where(first, pvl1, pvl0), C, 1)
                acc_st[pi, sel, :] = jnp.where(first, pvl0, pvl1)
    m_all = m_st[0]
    for pi in range(1, len(DIL_PATTERNS)):
        m_all = jnp.maximum(m_all, m_st[pi])
    num = jnp.zeros((tq, LANES), F32)
    den = jnp.zeros((tq, LANES), F32)
    for pi in range(len(DIL_PATTERNS)):
        w = jnp.exp(m_st[pi] - m_all)
        num = num + w * acc_st[pi]
        den = den + w * l_st[pi]
    o_ref[0, 0] = num / den


def _dilated_attention(q, k, v):
    _, B, S, _ = q.shape
    t = DIL_TILE
    nq = S // t
    cur = pl.BlockSpec((1, 1, t, LANES), lambda b, p, i: (p, b, i, 0))
    prev = pl.BlockSpec((1, 1, t, LANES), lambda b, p, i: (p, b, jnp.maximum(i - 1, 0), 0))
    nxt = pl.BlockSpec((1, 1, t, LANES), lambda b, p, i: (p, b, jnp.minimum(i + 1, nq - 1), 0))
    return pl.pallas_call(
        functools.partial(_dil_kernel, seq=S), grid=(B, 2, nq),
        in_specs=[cur, prev, cur, nxt, prev, cur, nxt],
        out_specs=cur, out_shape=jax.ShapeDtypeStruct(q.shape, F32),
        scratch_shapes=[pltpu.VMEM((3 * t, LANES), F32), pltpu.VMEM((3 * t, LANES), F32),
                        pltpu.VMEM((3, t, LANES), F32), pltpu.VMEM((3, t, LANES), F32), pltpu.VMEM((3, t, LANES), F32)],
        compiler_params=_params(("parallel", "parallel", "parallel")), name="dilated_attn",
    )(q, k, k, k, v, v, v)


def _dot2(a, b):
    a_hi, a_lo = _split2(a)
    b_hi, b_lo = _split2(b)
    return _dot(a_hi, b_hi) + _dot(a_hi, b_lo) + _dot(a_lo, b_hi)


def _exact_rhs_dot(x, m):
    hi, mid, lo = _split3(x)
    return _dot(hi, m) + _dot(mid, m) + _dot(lo, m)


def _ssm_ops_kernel(lamr_row, lami_row, ldt_row, lamr_col, lami_col, ldt_col, brt_ref, bit_ref, crt_ref, cit_ref,
                    t_ref, p_ref, q_ref, al_ref):
    L, N, P = SSM_CHUNK, SSM_STATE, SSM_GROUP
    K = L * P
    lane_j = lax.broadcasted_iota(jnp.int32, (L, K), 1) // P
    row_j = lax.broadcasted_iota(jnp.int32, (L, K), 0)
    sel_r = lax.broadcasted_iota(jnp.int32, (K, L), 0) // P
    sel_j = lax.broadcasted_iota(jnp.int32, (K, L), 1)
    jcol = lax.broadcasted_iota(jnp.int32, (N, L), 1).astype(F32)
    jrow = lax.broadcasted_iota(jnp.int32, (L, N), 0).astype(F32)
    kcat, p_parts, q_parts, al_parts = [], [], [], []
    for d in range(2):
        expand = jnp.where(row_j == (lane_j if d == 0 else L - 1 - lane_j), 1.0, 0.0).astype(BF16)
        select = jnp.where(sel_j == (L - 1 - sel_r if d == 0 else sel_r), 1.0, 0.0).astype(BF16)
        lr_c, li_c, dt_c = lamr_col[d, 0], lami_col[d, 0], jnp.exp(ldt_col[d, 0])
        pmag = jnp.exp(jcol * (lr_c * dt_c))
        pr0 = _exact_rhs_dot(pmag * jnp.cos(jcol * (li_c * dt_c)), expand)
        pi0 = _exact_rhs_dot(pmag * jnp.sin(jcol * (li_c * dt_c)), expand)
        mag_c = jnp.exp(lr_c * dt_c)
        ar_c, ai_c = mag_c * jnp.cos(li_c * dt_c), mag_c * jnp.sin(li_c * dt_c)
        pr1, pi1 = ar_c * pr0 - ai_c * pi0, ar_c * pi0 + ai_c * pr0
        cr = jnp.concatenate([crt_ref[d, 0]] * (K // LANES), axis=1)
        ci = jnp.concatenate([cit_ref[d, 0]] * (K // LANES), axis=1)
        lr_r, li_r, dt_r = lamr_row[d, 0], lami_row[d, 0], jnp.exp(ldt_row[d, 0])
        mag_r = jnp.exp(lr_r * dt_r)
        ar_r, ai_r = mag_r * jnp.cos(li_r * dt_r), mag_r * jnp.sin(li_r * dt_r)
        den = lr_r * lr_r + li_r * li_r
        fr = ((ar_r - 1.0) * lr_r + ai_r * li_r) / den
        fi = (ai_r * lr_r - (ar_r - 1.0) * li_r) / den
        bbr = fr * brt_ref[d, 0] - fi * bit_ref[d, 0]
        bbi = fr * bit_ref[d, 0] + fi * brt_ref[d, 0]
        kcat.append(_dot2(bbr, pr0 * cr - pi0 * ci) - _dot2(bbi, pr0 * ci + pi0 * cr))
        q_parts.append((pr1 * cr - pi1 * ci, -(pi1 * cr + pr1 * ci)))
        rmag = jnp.exp(jrow * (lr_r * dt_r))
        prr = rmag * jnp.cos(jrow * (li_r * dt_r))
        pir = rmag * jnp.sin(jrow * (li_r * dt_r))
        prs, pis = _exact_lhs_dot(select, prr), _exact_lhs_dot(select, pir)
        bbr_t = jnp.concatenate([bbr] * L, axis=0)
        bbi_t = jnp.concatenate([bbi] * L, axis=0)
        p_parts.append((prs * bbr_t - pis * bbi_t, prs * bbi_t + pis * bbr_t))
        al_parts.append((ar_r * prr[L - 1:L] - ai_r * pir[L - 1:L], ar_r * pir[L - 1:L] + ai_r * prr[L - 1:L]))
    zeros = jnp.zeros((P, K), F32)
    ksum = (jnp.concatenate([zeros, kcat[0]], axis=1)
            + pltpu.roll(jnp.concatenate([kcat[1], zeros], axis=1), P, 1))
    per_tile = LANES // P
    for b in range(per_tile):
        rolled = ksum if b == 0 else pltpu.roll(ksum, P * b, 1)
        for a in range(L // per_tile):
            r = per_tile * a + b
            t_ref[0, P * r:P * (r + 1), :] = rolled[:, K - LANES * a:2 * K - LANES * a].astype(BF16)
    p_ref[0] = jnp.concatenate([p_parts[0][0], p_parts[1][0], p_parts[0][1], p_parts[1][1]], axis=1).astype(BF16)
    q_ref[0] = jnp.concatenate([q_parts[0][0], q_parts[1][0], q_parts[0][1], q_parts[1][1]], axis=0).astype(BF16)
    al_ref[0] = jnp.concatenate([jnp.concatenate([al_parts[0][0], al_parts[1][0]], axis=1),
                                 jnp.concatenate([al_parts[0][1], al_parts[1][1]], axis=1)], axis=0)


def _ssm_operators(lam_re, lam_im, log_dt, b_re, b_im, c_re, c_im):
    L, G, N, P = SSM_CHUNK, SSM_GROUPS, SSM_STATE, SSM_GROUP
    K = L * P
    rowf = lambda v: v.reshape(2, G, 1, N)
    colf = lambda v: v.reshape(2, G, N, 1)
    ldt = jnp.broadcast_to(log_dt[:, :, None], (2, G, N))
    tr = lambda v: jnp.swapaxes(v, 2, 3)
    ct = lambda v: jnp.tile(tr(v), (1, 1, 1, LANES // P))
    spec = lambda shape: pl.BlockSpec((2, 1) + shape, lambda g: (0, g, 0, 0))
    t_sum, p_cat, q_cat, al = pl.pallas_call(
        _ssm_ops_kernel, grid=(G,),
        in_specs=[spec((1, N))] * 3 + [spec((N, 1))] * 3 + [spec((P, N))] * 2 + [spec((N, LANES))] * 2,
        out_specs=[pl.BlockSpec((1, K, K), lambda g: (g, 0, 0)), pl.BlockSpec((1, K, 4 * N), lambda g: (g, 0, 0)),
                   pl.BlockSpec((1, 4 * N, K), lambda g: (g, 0, 0)), pl.BlockSpec((1, 2, 2 * N), lambda g: (g, 0, 0))],
        out_shape=[jax.ShapeDtypeStruct((G, K, K), BF16), jax.ShapeDtypeStruct((G, K, 4 * N), BF16),
                   jax.ShapeDtypeStruct((G, 4 * N, K), BF16), jax.ShapeDtypeStruct((G, 2, 2 * N), F32)],
        compiler_params=_params(("parallel",)), name="ssm_ops",
    )(rowf(lam_re), rowf(lam_im), rowf(ldt), colf(lam_re), colf(lam_im), colf(ldt),
      tr(b_re), tr(b_im), ct(c_re), ct(c_im))
    return t_sum, p_cat, q_cat, al[:, 0, :].reshape(1, G * 2 * N), al[:, 1, :].reshape(1, G * 2 * N)


def _ssm_state_in_kernel(u_ref, p_ref, xre_ref, xim_ref):
    x = _dot(u_ref[0], p_ref[0])
    xre_ref[...] = x[:, 0:LANES]
    xim_ref[...] = x[:, LANES:2 * LANES]


def _ssm_state_in(u_t, p_cat):
    G, M, K = u_t.shape
    tm = min(M, 256)
    return pl.pallas_call(
        _ssm_state_in_kernel, grid=(G, M // tm),
        in_specs=[pl.BlockSpec((1, tm, K), lambda g, m: (g, m, 0)),
                  pl.BlockSpec((1, K, 2 * LANES), lambda g, m: (g, 0, 0))],
        out_specs=[pl.BlockSpec((tm, LANES), lambda g, m: (m, g))] * 2,
        out_shape=[jax.ShapeDtypeStruct((M, G * LANES), F32)] * 2,
        compiler_params=_params(("parallel", "parallel")), name="ssm_state_in",
    )(u_t, p_cat)


def _ssm_scan_kernel(xre_ref, xim_ref, are_ref, aim_ref, hfre_ref, hfim_ref, hbre_ref, hbim_ref):
    nc = xre_ref.shape[1]
    width = xre_ref.shape[2]
    ar = are_ref[...]
    ai = aim_ref[...]
    lane = lax.broadcasted_iota(jnp.int32, (1, width), 1)
    is_fwd = (lane % LANES) < SSM_STATE

    def step(i, carry):
        hr, hi = carry
        ib = nc - 1 - i
        hfre_ref[0, pl.ds(i, 1), :] = hr
        hfim_ref[0, pl.ds(i, 1), :] = hi
        hbre_ref[0, pl.ds(ib, 1), :] = hr
        hbim_ref[0, pl.ds(ib, 1), :] = hi
        xr = jnp.where(is_fwd, xre_ref[0, pl.ds(i, 1), :], xre_ref[0, pl.ds(ib, 1), :])
        xi = jnp.where(is_fwd, xim_ref[0, pl.ds(i, 1), :], xim_ref[0, pl.ds(ib, 1), :])
        return ar * hr - ai * hi + xr, ar * hi + ai * hr + xi

    zero = jnp.zeros((1, width), F32)
    lax.fori_loop(0, nc, step, (zero, zero))


def _ssm_scan(xre, xim, al_re, al_im):
    B, nc, width = xre.shape
    blk = pl.BlockSpec((1, nc, width), lambda b: (b, 0, 0))
    vec = pl.BlockSpec((1, width), lambda b: (0, 0))
    return pl.pallas_call(
        _ssm_scan_kernel, grid=(B,),
        in_specs=[blk, blk, vec, vec], out_specs=[blk] * 4,
        out_shape=[jax.ShapeDtypeStruct((B, nc, width), F32)] * 4,
        compiler_params=_params(("parallel",)), name="ssm_scan",
    )(xre, xim, al_re, al_im)


def _ssm_out_kernel(u_ref, t_ref, hfre_ref, hfim_ref, hbre_ref, hbim_ref, q_ref, y_ref):
    tm = u_ref.shape[1]
    lane = lax.broadcasted_iota(jnp.int32, (tm, LANES), 1)
    is_fwd = lane < SSM_STATE
    h_re = jnp.where(is_fwd, hfre_ref[...], hbre_ref[...])
    h_im = jnp.where(is_fwd, hfim_ref[...], hbim_ref[...])
    h_cat = jnp.concatenate([h_re, h_im], axis=1).astype(BF16)
    y_ref[0] = _dot(u_ref[0], t_ref[0]) + _dot(h_cat, q_ref[0])


def _ssm_out(u_t, t_sum, hs, q_cat):
    G, M, K = u_t.shape
    tm = min(M, 256)
    hblk = pl.BlockSpec((tm, LANES), lambda g, m: (m, g))
    return pl.pallas_call(
        _ssm_out_kernel, grid=(G, M // tm),
        in_specs=[pl.BlockSpec((1, tm, K), lambda g, m: (g, m, 0)),
                  pl.BlockSpec((1, K, K), lambda g, m: (g, 0, 0)),
                  hblk, hblk, hblk, hblk,
                  pl.BlockSpec((1, 2 * LANES, K), lambda g, m: (g, 0, 0))],
        out_specs=pl.BlockSpec((1, tm, K), lambda g, m: (g, m, 0)),
        out_shape=jax.ShapeDtypeStruct((G, M, K), F32),
        compiler_params=_params(("parallel", "parallel")), name="ssm_out",
    )(u_t, t_sum, *hs, q_cat)


def _s5_scan(u, ops):
    t_sum, p_cat, q_cat, al_re, al_im = ops
    B, S, _ = u.shape
    L, G, P = SSM_CHUNK, SSM_GROUPS, SSM_GROUP
    nc = S // L
    u_t = u.reshape(B, nc, L, G, P).transpose(3, 0, 1, 2, 4).reshape(G, B * nc, L * P).astype(BF16)
    xre, xim = _ssm_state_in(u_t, p_cat)
    width = G * LANES
    hs = _ssm_scan(xre.reshape(B, nc, width), xim.reshape(B, nc, width), al_re, al_im)
    hfre, hfim, hbre, hbim = (h.reshape(B * nc, width) for h in hs)
    y = _ssm_out(u_t, t_sum, (hfre, hfim, hbre, hbim), q_cat)
    return y.reshape(G, B, nc, L, P).transpose(1, 2, 3, 0, 4).reshape(B, S, G * P)


def _na_bias_kernel(rpb_ref, out_ref):
    h = pl.program_id(0)
    n_dr, n_dc = 2 * NA_WIN_R - 1, 2 * NA_WIN_C - 1
    qc = lax.broadcasted_iota(jnp.int32, (GRID_W, GRID_W), 0)
    kc = lax.broadcasted_iota(jnp.int32, (GRID_W, GRID_W), 1)
    dc = jnp.clip(kc - qc + (NA_WIN_C - 1), 0, n_dc - 1)
    cstart = jnp.clip(qc - NA_WIN_C // 2, 0, GRID_W - NA_WIN_C)
    col_ok = (kc >= cstart) & (kc < cstart + NA_WIN_C)
    base = []
    for dr in range(n_dr):
        acc = jnp.zeros((GRID_W, GRID_W), F32)
        for d in range(n_dc):
            acc = jnp.where(dc == d, rpb_ref[(h * n_dr + dr) * n_dc + d], acc)
        base.append(jnp.where(col_ok, acc, NEG))
    for off in range(NA_WIN_R):
        out_ref[0, off] = jnp.concatenate([base[w - off + NA_WIN_R - 1] for w in range(NA_WIN_R)], axis=1)


def _na_bias_table(rpb):
    return pl.pallas_call(
        _na_bias_kernel, grid=(HEADS,),
        in_specs=[pl.BlockSpec(memory_space=pltpu.SMEM)],
        out_specs=pl.BlockSpec((1, NA_WIN_R, GRID_W, NA_WIN_R * GRID_W), lambda h: (h, 0, 0, 0)),
        out_shape=jax.ShapeDtypeStruct((HEADS, NA_WIN_R, GRID_W, NA_WIN_R * GRID_W), F32),
        compiler_params=_params(("parallel",)), name="na_bias",
    )(rpb.astype(F32).reshape(-1))


def _na_kernel(q_ref, kp_ref, kc_ref, kn_ref, vp_ref, vc_ref, vn_ref, bias_ref, o_ref, kbuf, vbuf, *, rows):
    rb = pl.program_id(1)
    Wg = GRID_W
    C = HEAD_DIM
    halo = NA_HALO_ROWS * Wg
    cur = NA_ROWS_PER_STEP * Wg
    win = NA_WIN_R * Wg
    kbuf[0:halo, :] = kp_ref[0]
    kbuf[halo:halo + cur, :] = kc_ref[0]
    kbuf[halo + cur:2 * halo + cur, :] = kn_ref[0]
    vbuf[0:halo, :] = vp_ref[0]
    vbuf[halo:halo + cur, :] = vc_ref[0]
    vbuf[halo + cur:2 * halo + cur, :] = vn_ref[0]
    r0 = rb * NA_ROWS_PER_STEP
    group = 2
    for j0 in range(0, NA_ROWS_PER_STEP, group):
        scores, values = [], []
        for j in range(j0, j0 + group):
            r = r0 + j
            rs = jnp.clip(r - NA_WIN_R // 2, 0, rows - NA_WIN_R)
            off = r - rs
            start = pl.multiple_of((rs - r0 + NA_HALO_ROWS) * Wg, Wg)
            kw = kbuf[pl.ds(start, win), :]
            values.append(vbuf[pl.ds(start, win), :])
            for h in range(HEADS):
                sl = slice(h * C, (h + 1) * C)
                scores.append(_dot_nt(q_ref[0, j * Wg:(j + 1) * Wg, sl], kw[:, sl]) + bias_ref[h, off])
        s = jnp.concatenate(scores, axis=0)
        p = jnp.exp(s - jnp.max(s, axis=-1, keepdims=True))
        inv_l = 1.0 / jnp.sum(p, axis=-1, keepdims=True)
        pb = p.astype(BF16)
        for g in range(group):
            outs = []
            for h in range(HEADS):
                blk = slice((g * HEADS + h) * Wg, (g * HEADS + h + 1) * Wg)
                outs.append(_dot(pb[blk], values[g][:, h * C:(h + 1) * C]) * inv_l[blk])
            o_ref[0, (j0 + g) * Wg:(j0 + g + 1) * Wg, :] = jnp.concatenate(outs, axis=1)


def _neighbourhood_attention(q, k, v, bias):
    B, S, W = q.shape
    rows = S // GRID_W
    cur = NA_ROWS_PER_STEP * GRID_W
    halo = NA_HALO_ROWS * GRID_W
    nrb = S // cur
    per = cur // halo
    nh = S // halo
    blk = pl.BlockSpec((1, cur, W), lambda b, i: (b, i, 0))
    prev = pl.BlockSpec((1, halo, W), lambda b, i: (b, jnp.maximum(i * per - 1, 0), 0))
    nxt = pl.BlockSpec((1, halo, W), lambda b, i: (b, jnp.minimum((i + 1) * per, nh - 1), 0))
    bspec = pl.BlockSpec(bias.shape, lambda b, i: (0, 0, 0, 0))
    return pl.pallas_call(
        functools.partial(_na_kernel, rows=rows), grid=(B, nrb),
        in_specs=[blk, prev, blk, nxt, prev, blk, nxt, bspec],
        out_specs=blk, out_shape=jax.ShapeDtypeStruct((B, S, W), F32),
        scratch_shapes=[pltpu.VMEM((cur + 2 * halo, W), BF16), pltpu.VMEM((cur + 2 * halo, W), BF16)],
        compiler_params=_params(("parallel", "parallel")), name="neighbourhood_attn",
    )(q, k, k, k, v, v, v, bias)


def _post_kernel(x_ref, of_ref, ob_ref, gate_ref, yb_ref, ys_ref, u_ref, yd_ref,
                 dnw_ref, bd_ref, dskip_ref, gluw_ref, glub_ref, wout_ref, ln2_ref, w1_ref, w2_ref, fnw_ref,
                 out_ref, *, final):
    x = x_ref[...]
    o = of_ref[...] + ob_ref[...]
    ms = _head_sums(o * o, bd_ref[...]) * (1.0 / HEAD_DIM)
    g = gate_ref[...]
    y_a = o * lax.rsqrt(ms + EPS) * dnw_ref[...] * (g * _sigmoid(g))
    y = ys_ref[...] + dskip_ref[...] * u_ref[...]
    z = 0.5 * y * (1.0 + jnp.tanh(math.sqrt(2.0 / math.pi) * (y + 0.044715 * (y * y * y))))
    y_c = z * _sigmoid(_dot(z.astype(BF16), gluw_ref[...]) + glub_ref[...])
    mix = jnp.concatenate([y_a, yb_ref[0], yb_ref[1], y_c, yd_ref[...]], axis=1).astype(BF16)
    x1 = x + _dot(mix, wout_ref[...])
    hb = _rms(x1, ln2_ref[...]).astype(BF16)
    ffc = D_MODEL
    acc = x1
    for c in range(D_FF // ffc):
        a = jnp.maximum(_dot(hb, w1_ref[:, c * ffc:(c + 1) * ffc]), 0.0)
        acc = acc + _dot((a * a).astype(BF16), w2_ref[c * ffc:(c + 1) * ffc, :])
    if final:
        acc = _rms(acc, fnw_ref[...])
    out_ref[...] = acc


def _post(x, o_f, o_b, gate, y_b, y_s, u, y_d, wts, final):
    T = x.shape[0]
    tm = min(TOKEN_TILE, T)
    W = WIDTH
    row = lambda n: pl.BlockSpec((tm, n), lambda i: (i, 0))
    const = lambda a: pl.BlockSpec(a.shape, lambda i: (0, 0), pipeline_mode=pl.Buffered(1))
    return pl.pallas_call(
        functools.partial(_post_kernel, final=final), grid=(T // tm,),
        in_specs=([row(D_MODEL)] + [row(W)] * 3 + [pl.BlockSpec((2, tm, LANES), lambda i: (0, i, 0))] + [row(W)] * 3
                  + [const(a) for a in wts]),
        out_specs=row(D_MODEL), out_shape=jax.ShapeDtypeStruct((T, D_MODEL), F32),
        compiler_params=_params(("parallel",)), name="post",
    )(x, o_f, o_b, gate, y_b, y_s, u, y_d, *wts)


def _rope_tables(seq):
    inv_freq = ROPE_THETA ** (-jnp.arange(0, HEAD_DIM, 2, dtype=F32) / HEAD_DIM)
    ang = jnp.arange(seq, dtype=F32)[:, None] * inv_freq[None, :]
    cos, sin = jnp.cos(ang), jnp.sin(ang)
    cos_t = jnp.tile(jnp.concatenate([cos, cos], axis=1), (1, HEADS))
    sin_t = jnp.tile(jnp.concatenate([-sin, sin], axis=1), (1, HEADS))
    return cos_t, sin_t


def _layer_weights(i, p):
    W = WIDTH
    w = p['w_in'][i]
    o_gate, o_a, o_b, o_dil, o_ssm, o_na = 3 * W, 4 * W, 4 * W + 2 * HEADS, 4 * W + 4 * HEADS, 7 * W + 4 * HEADS, 8 * W + 4 * HEADS
    pad = jnp.zeros((D_MODEL, LANES - 4 * HEADS), F32)
    w_in = jnp.concatenate([w[:, 0:o_a], w[:, o_dil:o_na + 3 * W], w[:, o_a:o_dil], pad], axis=1).astype(BF16)
    assert w_in.shape[1] == IN_COLS_PADDED
    lane_pad = lambda v: jnp.concatenate([v.reshape(1, -1).astype(F32), jnp.zeros((1, LANES - v.size), F32)], axis=1)
    idx = np.arange(W)
    bd = jnp.asarray((idx[:, None] // HEAD_DIM) == (idx[None, :] // HEAD_DIM), BF16)
    row = lambda v: v.reshape(1, -1).astype(F32)
    return dict(
        ln1=row(p['ln1_w'][i]), w_in=w_in,
        conv_w=p['dn_conv_w'][i].astype(F32), alog=lane_pad(p['dn_a_log'][i]), dtb=lane_pad(p['dn_dt_bias'][i]),
        bd=bd,
        ssm=_ssm_operators(*(p[n][i].astype(F32) for n in
                             ('ssm_lam_re', 'ssm_lam_im', 'ssm_log_dt', 'ssm_b_re', 'ssm_b_im', 'ssm_c_re', 'ssm_c_im'))),
        na_bias=_na_bias_table(p['na_rpb'][i]),
        post=(row(jnp.tile(p['dn_norm_w'][i], HEADS)), bd, row(p['ssm_d'][i]), p['ssm_glu_w'][i].astype(BF16),
              row(p['ssm_glu_b'][i]), p['w_out'][i].astype(BF16), row(p['ln2_w'][i]),
              p['w_ff1'][i].astype(BF16), p['w_ff2'][i].astype(BF16), row(p['final_norm_w'])),
    )


def _trunk(x, layers):
    B, S, D = x.shape
    T = B * S
    cos_t, sin_t = _rope_tables(S)
    xt = x.reshape(T, D)
    seq3 = lambda a: a.reshape(B, S, a.shape[-1])
    flat = lambda a: a.reshape(T, a.shape[-1])
    for i, lw in enumerate(layers):
        z_dn, z_gate, z_ab, dq, dk, dv, z_ssm, nq, nk, nv = _in_proj(xt, lw['ln1'], lw['w_in'], cos_t, sin_t, S)
        q, k, v, gb = _dn_prep(seq3(z_dn), seq3(z_ab), lw['conv_w'], lw['alog'], lw['dtb'], lw['bd'])
        o_f = _dn_scan(q, k, v, gb, False)
        o_b = _dn_scan(q, k, v, gb, True)
        pairs = lambda a: a.reshape(2, B, S, LANES)
        y_b = _dilated_attention(pairs(dq), pairs(dk), pairs(dv)).reshape(2, T, LANES)
        y_s = _s5_scan(seq3(z_ssm), lw['ssm'])
        y_d = _neighbourhood_attention(seq3(nq), seq3(nk), seq3(nv), lw['na_bias'])
        xt = _post(xt, flat(o_f), flat(o_b), z_gate, y_b, flat(y_s), z_ssm, flat(y_d),
                   lw['post'], final=(i == len(layers) - 1))
    return xt.reshape(B, S, D)


def kernel(x_prompt, x_sample, ln1_w, w_in, dn_conv_w, dn_a_log, dn_dt_bias, dn_norm_w, ssm_lam_re, ssm_lam_im,
           ssm_log_dt, ssm_b_re, ssm_b_im, ssm_c_re, ssm_c_im, ssm_d, ssm_glu_w, ssm_glu_b, na_rpb, w_out,
           ln2_w, w_ff1, w_ff2, final_norm_w):
    p = dict(ln1_w=ln1_w, w_in=w_in, dn_conv_w=dn_conv_w, dn_a_log=dn_a_log, dn_dt_bias=dn_dt_bias,
             dn_norm_w=dn_norm_w, ssm_lam_re=ssm_lam_re, ssm_lam_im=ssm_lam_im, ssm_log_dt=ssm_log_dt,
             ssm_b_re=ssm_b_re, ssm_b_im=ssm_b_im, ssm_c_re=ssm_c_re, ssm_c_im=ssm_c_im, ssm_d=ssm_d,
             ssm_glu_w=ssm_glu_w, ssm_glu_b=ssm_glu_b, na_rpb=na_rpb, w_out=w_out, ln2_w=ln2_w,
             w_ff1=w_ff1, w_ff2=w_ff2, final_norm_w=final_norm_w)
    layers = [_layer_weights(i, p) for i in range(DEPTH)]
    return (_trunk(x_prompt, layers), _trunk(x_sample, layers))
```

```python
import functools
import math

import numpy as np
import jax
import jax.numpy as jnp
from jax import lax
from jax.experimental import pallas as pl
from jax.experimental.pallas import tpu as pltpu

F32 = jnp.float32
BF16 = jnp.bfloat16

D_MODEL = 1024
DEPTH = 2
HEAD_DIM = 64
HEADS = 4
WIDTH = HEADS * HEAD_DIM
DN_CHUNK = 64
CONV_WIDTH = 5
DIL_PATTERNS = ((128, 1), (512, 4), (2048, 16))
SSM_GROUP = 16
SSM_GROUPS = 16
SSM_STATE = 64
SSM_CHUNK = 64
GRID_W = 64
NA_WIN_R = 8
NA_WIN_C = 16
D_FF = 4 * D_MODEL
ROPE_THETA = 10000.0
EPS = 1e-6
NEG = -1e30

LANES = 128
TOKEN_TILE = 512
DIL_TILE = 1024
DIL_BAND = 64
NA_ROWS_PER_STEP = 8
NA_HALO_ROWS = 4
IN_COLS_PADDED = 23 * LANES
VMEM_LIMIT = 56 * 2**20


def _params(sem):
    return pltpu.CompilerParams(dimension_semantics=sem, vmem_limit_bytes=VMEM_LIMIT)


def _dot(a, b):
    return jnp.dot(a, b, preferred_element_type=F32)


def _dot_nt(a, b):
    return lax.dot_general(a, b, (((1,), (1,)), ((), ())), preferred_element_type=F32)


def _bmm(a, b):
    return lax.dot_general(a, b, (((2,), (1,)), ((0,), (0,))), preferred_element_type=F32)


def _bmm_nt(a, b):
    return lax.dot_general(a, b, (((2,), (2,)), ((0,), (0,))), preferred_element_type=F32)


def _split2(x):
    hi = x.astype(BF16)
    lo = (x - hi.astype(F32)).astype(BF16)
    return hi, lo


def _split3(x):
    hi = x.astype(BF16)
    r = x - hi.astype(F32)
    mid = r.astype(BF16)
    lo = (r - mid.astype(F32)).astype(BF16)
    return hi, mid, lo


def _exact_lhs_dot(m, x):
    hi, mid, lo = _split3(x)
    return _dot(m, hi) + _dot(m, mid) + _dot(m, lo)


def _head_sums(sq, bd):
    hi, lo = _split2(sq)
    return _dot(hi, bd) + _dot(lo, bd)


def _sigmoid(x):
    return 1.0 / (1.0 + jnp.exp(-x))


def _rms(x, w):
    ms = jnp.mean(x * x, axis=-1, keepdims=True)
    return x * lax.rsqrt(ms + EPS) * w


def _in_proj_kernel(x_ref, lnw_ref, w_ref, cos_ref, sin_ref,
                    dn_ref, gate_ref, ab_ref, dq_ref, dk_ref, dv_ref, ssm_ref, nq_ref, nk_ref, nv_ref):
    h = _rms(x_ref[...], lnw_ref[...]).astype(BF16)
    z = _dot(h, w_ref[...])
    tm = z.shape[0]
    W = WIDTH
    dn_ref[...] = z[:, 0:3 * W]
    gate_ref[...] = z[:, 3 * W:4 * W]
    cos = cos_ref[...]
    sin = sin_ref[...]
    lane = lax.broadcasted_iota(jnp.int32, (tm, W), 1)
    first_half = (lane % HEAD_DIM) < (HEAD_DIM // 2)

    def rope(t):
        rot = jnp.where(first_half, pltpu.roll(t, W - HEAD_DIM // 2, 1), pltpu.roll(t, HEAD_DIM // 2, 1))
        return t * cos + rot * sin

    scale = HEAD_DIM ** -0.5
    for ref, t in ((dq_ref, rope(z[:, 4 * W:5 * W]) * scale), (dk_ref, rope(z[:, 5 * W:6 * W])),
                   (dv_ref, z[:, 6 * W:7 * W])):
        ref[0] = t[:, 0:LANES]
        ref[1] = t[:, LANES:2 * LANES]
    ssm_ref[...] = z[:, 7 * W:8 * W]
    nq_ref[...] = (z[:, 8 * W:9 * W] * scale).astype(BF16)
    nk_ref[...] = z[:, 9 * W:10 * W].astype(BF16)
    nv_ref[...] = z[:, 10 * W:11 * W].astype(BF16)
    ab_ref[...] = z[:, 11 * W:11 * W + LANES]


def _in_proj(x, lnw, w, cos_t, sin_t, seq):
    T = x.shape[0]
    tm = min(TOKEN_TILE, seq)
    per_seq = seq // tm
    W = WIDTH
    row = lambda n: pl.BlockSpec((tm, n), lambda i: (i, 0))
    const = lambda a: pl.BlockSpec(a.shape, lambda i: (0, 0))
    tab = pl.BlockSpec((tm, W), lambda i: (i % per_seq, 0))
    pair = pl.BlockSpec((2, tm, LANES), lambda i: (0, i, 0))
    pair_shape = jax.ShapeDtypeStruct((2, T, LANES), F32)
    out_shapes = [
        jax.ShapeDtypeStruct((T, 3 * W), F32), jax.ShapeDtypeStruct((T, W), F32),
        jax.ShapeDtypeStruct((T, LANES), F32),
        pair_shape, pair_shape, pair_shape,
        jax.ShapeDtypeStruct((T, W), F32),
        jax.ShapeDtypeStruct((T, W), BF16), jax.ShapeDtypeStruct((T, W), BF16), jax.ShapeDtypeStruct((T, W), BF16),
    ]
    out_specs = [row(3 * W), row(W), row(LANES), pair, pair, pair, row(W), row(W), row(W), row(W)]
    return pl.pallas_call(
        _in_proj_kernel, grid=(T // tm,),
        in_specs=[row(D_MODEL), const(lnw), const(w), tab, tab],
        out_specs=out_specs, out_shape=out_shapes,
        compiler_params=_params(("parallel",)), name="in_proj",
    )(x, lnw, w, cos_t, sin_t)


def _dn_prep_kernel(zc_ref, zp_ref, zn_ref, ab_ref, cw_ref, alog_ref, dtb_ref, bd_ref,
                    q_ref, k_ref, v_ref, gb_ref, buf):
    i = pl.program_id(1)
    n = pl.num_programs(1)
    tb = zc_ref.shape[1]
    W = WIDTH
    half = CONV_WIDTH // 2
    buf[0:8, :] = jnp.where(i > 0, zp_ref[0], 0.0)
    buf[8:8 + tb, :] = zc_ref[0]
    buf[8 + tb:16 + tb, :] = jnp.where(i < n - 1, zn_ref[0], 0.0)
    cw = cw_ref[...]
    acc = buf[8 - half:8 - half + tb, :] * cw[0:1, :]
    for j in range(1, CONV_WIDTH):
        acc = acc + buf[8 - half + j:8 - half + j + tb, :] * cw[j:j + 1, :]
    y = acc * _sigmoid(acc)
    bd = bd_ref[...]
    q = y[:, 0:W]
    k = y[:, W:2 * W]
    q_ref[0] = q * lax.rsqrt(_head_sums(q * q, bd) + EPS) * (HEAD_DIM ** -0.5)
    k_ref[0] = k * lax.rsqrt(_head_sums(k * k, bd) + EPS)
    v_ref[0] = y[:, 2 * W:3 * W]
    ab = ab_ref[0]
    t = ab + dtb_ref[...]
    softplus = jnp.maximum(t, 0.0) + jnp.log(1.0 + jnp.exp(-jnp.abs(t)))
    g = -jnp.exp(alog_ref[...]) * softplus
    lane = lax.broadcasted_iota(jnp.int32, ab.shape, 1)
    gb_ref[0] = jnp.where(lane < 2 * HEADS, g, _sigmoid(ab))


def _dn_prep(z_dn, z_ab, conv_w, alog, dtb, bd):
    B, S, C = z_dn.shape
    tb = min(TOKEN_TILE, S)
    nb = S // tb
    r8 = tb // 8
    W = WIDTH
    blk = lambda n: pl.BlockSpec((1, tb, n), lambda b, i: (b, i, 0))
    const = lambda a: pl.BlockSpec(a.shape, lambda b, i: (0, 0))
    prev = pl.BlockSpec((1, 8, C), lambda b, i: (b, jnp.maximum(i * r8 - 1, 0), 0))
    nxt = pl.BlockSpec((1, 8, C), lambda b, i: (b, jnp.minimum((i + 1) * r8, S // 8 - 1), 0))
    return pl.pallas_call(
        _dn_prep_kernel, grid=(B, nb),
        in_specs=[blk(C), prev, nxt, blk(LANES), const(conv_w), const(alog), const(dtb), const(bd)],
        out_specs=[blk(W), blk(W), blk(W), blk(LANES)],
        out_shape=[jax.ShapeDtypeStruct((B, S, W), F32)] * 3 + [jax.ShapeDtypeStruct((B, S, LANES), F32)],
        scratch_shapes=[pltpu.VMEM((tb + 16, C), F32)],
        compiler_params=_params(("parallel", "parallel")), name="dn_prep",
    )(z_dn, z_dn, z_dn, z_ab, conv_w, alog, dtb, bd)


def _dn_scan_kernel(qf_ref, kf_ref, vf_ref, gf_ref, qr_ref, kr_ref, vr_ref, gr_ref, of_ref, or_ref,
                    s_scr, u_scr, w_scr, qg_scr, at_scr, kdt_scr, *, nc):
    C = DN_CHUNK
    tb = nc * C
    H = HEADS
    assert LANES == 2 * C

    @pl.when(pl.program_id(1) == 0)
    def _():
        s_scr[...] = jnp.zeros_like(s_scr)

    r = lax.broadcasted_iota(jnp.int32, (tb, tb), 0)
    c = lax.broadcasted_iota(jnp.int32, (tb, tb), 1)
    same = (r // C) == (c // C)
    ri = lax.broadcasted_iota(jnp.int32, (LANES, LANES), 0)
    ci = lax.broadcasted_iota(jnp.int32, (LANES, LANES), 1)
    eye128 = jnp.where(ri == ci, 1.0, 0.0).astype(BF16)
    same_head = (ri < C) == (ci < C)
    lane = lax.broadcasted_iota(jnp.int32, (tb, LANES), 1)
    first = lane < C
    ii = lax.broadcasted_iota(jnp.int32, (tb, LANES), 0) % C
    jj = lane % C
    lane_c = lax.broadcasted_iota(jnp.int32, (1, LANES), 1)

    def block_diag(x):
        ln = lax.broadcasted_iota(jnp.int32, x.shape, x.ndim - 1)
        zero = jnp.zeros_like(x)
        return jnp.concatenate([jnp.where(ln < C, x, zero), jnp.where(ln >= C, x, zero)], axis=x.ndim - 2)

    pick = lambda a, col: jnp.where(first, a[:, col:col + 1], a[:, col + 1:col + 2])
    parts = {name: [] for name in ('q', 'k', 'v', 'gam', 'gtot', 'beta', 'grow', 'incl', 'strict')}
    gtots = []
    for d, (q_ref, k_ref, v_ref, gb_ref) in enumerate(((qf_ref, kf_ref, vf_ref, gf_ref),
                                                       (qr_ref, kr_ref, vr_ref, gr_ref))):
        rev = d == 1
        gb = gb_ref[0]
        cum = same & ((c >= r) if rev else (c <= r))
        gam = _exact_lhs_dot(jnp.where(cum, 1.0, 0.0).astype(BF16), gb)
        gam3 = gam.reshape(nc, C, LANES)
        last = gam3[:, 0:1, :] if rev else gam3[:, C - 1:C, :]
        gtot = jnp.broadcast_to(last, (nc, C, LANES)).reshape(tb, LANES)
        gtots.append(gtot)
        g_hi, g_mid, g_lo = _split3(gam)
        gam_t = _dot_nt(eye128, g_hi) + _dot_nt(eye128, g_mid) + _dot_nt(eye128, g_lo)
        for pr in range(H // 2):
            ch0 = d * H + 2 * pr
            tile = slice(pr * LANES, (pr + 1) * LANES)
            parts['q'].append(q_ref[0, :, tile])
            parts['k'].append(k_ref[0, :, tile])
            parts['v'].append(v_ref[0, :, tile])
            parts['gam'].append(pick(gam, ch0))
            parts['gtot'].append(pick(gtot, ch0))
            parts['beta'].append(pick(gb, 2 * H + ch0))
            parts['grow'].append(jnp.concatenate(
                [jnp.concatenate([jnp.broadcast_to(gam_t[ch0:ch0 + 1, cc * C:(cc + 1) * C], (C, C)),
                                  jnp.broadcast_to(gam_t[ch0 + 1:ch0 + 2, cc * C:(cc + 1) * C], (C, C))], axis=1)
                 for cc in range(nc)], axis=0))
            parts['incl'].append((jj >= ii) if rev else (jj <= ii))
            parts['strict'].append((jj > ii) if rev else (jj < ii))
    stk = {name: jnp.concatenate(vals, axis=0) for name, vals in parts.items()}
    slots = 2 * (H // 2)
    rows_all = slots * tb
    chunks = lambda x: x.astype(BF16).reshape(slots * nc, C, x.shape[-1])
    ii_all = jnp.concatenate([ii] * slots, axis=0)
    jj_all = jnp.concatenate([jj] * slots, axis=0)
    qp, kp, vp, gam_c, beta = stk['q'], stk['k'], stk['v'], stk['gam'], stk['beta']
    dec = jnp.exp(jnp.where(stk['incl'], gam_c - stk['grow'], NEG))
    kb = kp * beta
    k_bd = block_diag(chunks(kp))
    a_raw = _bmm_nt(chunks(kb), k_bd).reshape(rows_all, LANES)
    qk = _bmm_nt(chunks(qp), k_bd).reshape(rows_all, LANES)
    lm = jnp.where(stk['strict'], a_raw * dec, 0.0)
    attn = qk * dec
    t_inv = jnp.where(ii_all == jj_all, 1.0, 0.0) - jnp.where((ii_all // 2) == (jj_all // 2), lm, 0.0)
    n = 2
    while n < C:
        off = ((ii_all // (2 * n)) == (jj_all // (2 * n))) & ((ii_all // n) != (jj_all // n))
        t3 = chunks(t_inv)
        bt = _bmm(chunks(jnp.where(off, lm, 0.0)), block_diag(t3))
        t_inv = t_inv - _bmm(t3, block_diag(bt.astype(BF16))).reshape(rows_all, LANES)
        n *= 2
    eg = jnp.exp(gam_c)
    rhs = jnp.concatenate([block_diag(chunks(vp * beta)), block_diag(chunks(kb * eg))], axis=2)
    sol = _bmm(chunks(t_inv), rhs).reshape(rows_all, 2 * LANES)
    u_scr[...] = sol[:, 0:LANES].reshape(slots, tb, LANES)
    w_scr[...] = sol[:, LANES:2 * LANES].astype(BF16).reshape(slots, tb, LANES)
    qg_scr[...] = (qp * eg).astype(BF16).reshape(slots, tb, LANES)
    at_scr[...] = attn.astype(BF16).reshape(slots, tb, LANES)
    kd = (kp * jnp.exp(stk['gtot'] - gam_c)).astype(BF16)
    for slot in range(slots):
        kdt_scr[slot] = _dot_nt(eye128, kd[slot * tb:(slot + 1) * tb]).astype(BF16)

    for step in range(nc):
        rng = []
        for slot in range(slots):
            cc = nc - 1 - step if slot >= H // 2 else step
            rng.append((cc * C, (cc + 1) * C))
        take = lambda ref: jnp.stack([ref[slot, lo:hi, :] for slot, (lo, hi) in enumerate(rng)], axis=0)
        s = s_scr[...]
        sb = s.astype(BF16)
        v_new = take(u_scr) - _bmm(take(w_scr), sb)
        vb = v_new.astype(BF16)
        o = _bmm(take(qg_scr), sb) + _bmm(take(at_scr), block_diag(vb))
        gl = []
        for slot, (lo, hi) in enumerate(rng):
            d, pr = divmod(slot, H // 2)
            ch0 = d * H + 2 * pr
            o_ref = or_ref if d == 1 else of_ref
            o_ref[0, lo:hi, pr * LANES:(pr + 1) * LANES] = o[slot]
            gt = gtots[d]
            gl.append(jnp.exp(jnp.where(lane_c < C, gt[lo:lo + 1, ch0:ch0 + 1], gt[lo:lo + 1, ch0 + 1:ch0 + 2])))
        kdt = jnp.stack([kdt_scr[slot, :, lo:hi] for slot, (lo, hi) in enumerate(rng)], axis=0)
        upd = _bmm(kdt, vb)
        s_scr[...] = s * jnp.stack(gl, axis=0) + jnp.where(same_head[None], upd, 0.0)


def _dn_scan(q, k, v, gb):
    B, S, W = q.shape
    tb = min(TOKEN_TILE, S)
    nb = S // tb
    nc = tb // DN_CHUNK
    fwd = lambda n: pl.BlockSpec((1, tb, n), lambda b, i: (b, i, 0))
    bwd = lambda n: pl.BlockSpec((1, tb, n), lambda b, i: (b, nb - 1 - i, 0))
    slots = HEADS
    return pl.pallas_call(
        functools.partial(_dn_scan_kernel, nc=nc), grid=(B, nb),
        in_specs=[fwd(W), fwd(W), fwd(W), fwd(LANES), bwd(W), bwd(W), bwd(W), bwd(LANES)],
        out_specs=[fwd(W), bwd(W)], out_shape=[jax.ShapeDtypeStruct((B, S, W), F32)] * 2,
        scratch_shapes=[pltpu.VMEM((slots, LANES, LANES), F32), pltpu.VMEM((slots, tb, LANES), F32),
                        pltpu.VMEM((slots, tb, LANES), BF16), pltpu.VMEM((slots, tb, LANES), BF16),
                        pltpu.VMEM((slots, tb, LANES), BF16), pltpu.VMEM((slots, LANES, tb), BF16)],
        compiler_params=_params(("parallel", "arbitrary")), name="dn_scan",
    )(q, k, v, gb, q, k, v, gb)


def _dil_kernel(q_ref, kp_ref, kc_ref, kn_ref, vp_ref, vc_ref, vn_ref, o_ref,
                kwin, vwin, m_st, l_st, acc_st, *, seq):
    i = pl.program_id(2)
    tq = DIL_TILE
    band = DIL_BAND
    C = HEAD_DIM
    assert LANES == 2 * C
    kwin[0:tq, :] = kp_ref[0, 0]
    kwin[tq:2 * tq, :] = kc_ref[0, 0]
    kwin[2 * tq:3 * tq, :] = kn_ref[0, 0]
    vwin[0:tq, :] = vp_ref[0, 0]
    vwin[tq:2 * tq, :] = vc_ref[0, 0]
    vwin[2 * tq:3 * tq, :] = vn_ref[0, 0]

    def rows(start, size, stride):
        return pl.ds(start, size) if stride == 1 else pl.ds(start, size, stride=stride)

    for pi, (window, dil) in enumerate(DIL_PATTERNS):
        assert window // (2 * dil) == band and band * dil <= tq
        qb = min(2 * band, tq // dil)
        nk = qb + 2 * band
        qa = lax.broadcasted_iota(jnp.int32, (qb, nk), 0)
        kj = lax.broadcasted_iota(jnp.int32, (qb, nk), 1)
        band_bias = jnp.where((kj >= qa) & (kj <= qa + 2 * band), 0.0, NEG)
        kcol = lax.broadcasted_iota(jnp.int32, (1, nk), 1)
        vlane = lax.broadcasted_iota(jnp.int32, (nk, LANES), 1)
        qlane = lax.broadcasted_iota(jnp.int32, (qb, LANES), 1)
        one = jnp.ones((nk, LANES), BF16)
        units = [(c, u) for c in range(dil) for u in range(tq // (dil * qb))]
        group = 4
        first = qlane < C
        for g0 in range(0, len(units), group):
            scores, values = [], []
            for c, u in units[g0:g0 + group]:
                q0 = c + dil * u * qb
                k0 = tq + q0 - dil * band
                pos = i * tq + (q0 - dil * band) + dil * kcol
                bias = band_bias + jnp.where((pos >= 0) & (pos < seq), 0.0, NEG)
                qr = q_ref[0, 0, rows(q0, qb, dil), :].astype(BF16)
                kr = kwin[rows(k0, nk, dil), :].astype(BF16)
                values.append(vwin[rows(k0, nk, dil), :].astype(BF16))
                for hh in range(LANES // C):
                    sl = slice(hh * C, (hh + 1) * C)
                    scores.append(_dot_nt(qr[:, sl], kr[:, sl]) + bias)
            s = jnp.concatenate(scores, axis=0)
            m = jnp.max(s, axis=-1, keepdims=True)
            pb = jnp.exp(s - m).astype(BF16)
            for g, (c, u) in enumerate(units[g0:g0 + group]):
                b0 = slice(2 * g * qb, (2 * g + 1) * qb)
                b1 = slice((2 * g + 1) * qb, (2 * g + 2) * qb)
                pvl0 = _dot(pb[b0], jnp.where(vlane < C, values[g], one))
                pvl1 = _dot(pb[b1], jnp.where(vlane >= C, values[g], one))
                sel = rows(c + dil * u * qb, qb, dil)
                m_st[pi, sel, :] = jnp.where(first, m[b0], m[b1])
                l_st[pi, sel, :] = pltpu.roll(jnp.where(first, pvl1, pvl0), C, 1)
                acc_st[pi, sel, :] = jnp.where(first, pvl0, pvl1)
    m_all = m_st[0]
    for pi in range(1, len(DIL_PATTERNS)):
        m_all = jnp.maximum(m_all, m_st[pi])
    num = jnp.zeros((tq, LANES), F32)
    den = jnp.zeros((tq, LANES), F32)
    for pi in range(len(DIL_PATTERNS)):
        w = jnp.exp(m_st[pi] - m_all)
        num = num + w * acc_st[pi]
        den = den + w * l_st[pi]
    o_ref[0, 0] = num / den


def _dilated_attention(q, k, v):
    _, B, S, _ = q.shape
    t = DIL_TILE
    nq = S // t
    cur = pl.BlockSpec((1, 1, t, LANES), lambda b, p, i: (p, b, i, 0))
    prev = pl.BlockSpec((1, 1, t, LANES), lambda b, p, i: (p, b, jnp.maximum(i - 1, 0), 0))
    nxt = pl.BlockSpec((1, 1, t, LANES), lambda b, p, i: (p, b, jnp.minimum(i + 1, nq - 1), 0))
    return pl.pallas_call(
        functools.partial(_dil_kernel, seq=S), grid=(B, 2, nq),
        in_specs=[cur, prev, cur, nxt, prev, cur, nxt],
        out_specs=cur, out_shape=jax.ShapeDtypeStruct(q.shape, F32),
        scratch_shapes=[pltpu.VMEM((3 * t, LANES), F32), pltpu.VMEM((3 * t, LANES), F32),
                        pltpu.VMEM((3, t, LANES), F32), pltpu.VMEM((3, t, LANES), F32), pltpu.VMEM((3, t, LANES), F32)],
        compiler_params=_params(("parallel", "parallel", "parallel")), name="dilated_attn",
    )(q, k, k, k, v, v, v)


def _dot2(a, b):
    a_hi, a_lo = _split2(a)
    b_hi, b_lo = _split2(b)
    return _dot(a_hi, b_hi) + _dot(a_hi, b_lo) + _dot(a_lo, b_hi)


def _exact_rhs_dot(x, m):
    hi, mid, lo = _split3(x)
    return _dot(hi, m) + _dot(mid, m) + _dot(lo, m)


def _ssm_ops_kernel(lamr_row, lami_row, ldt_row, lamr_col, lami_col, ldt_col, brt_ref, bit_ref, crt_ref, cit_ref,
                    t_ref, p_ref, q_ref, al_ref):
    L, N, P = SSM_CHUNK, SSM_STATE, SSM_GROUP
    K = L * P
    lane_j = lax.broadcasted_iota(jnp.int32, (L, K), 1) // P
    row_j = lax.broadcasted_iota(jnp.int32, (L, K), 0)
    sel_r = lax.broadcasted_iota(jnp.int32, (K, L), 0) // P
    sel_j = lax.broadcasted_iota(jnp.int32, (K, L), 1)
    jcol = lax.broadcasted_iota(jnp.int32, (N, L), 1).astype(F32)
    jrow = lax.broadcasted_iota(jnp.int32, (L, N), 0).astype(F32)
    kcat, p_parts, q_parts, al_parts = [], [], [], []
    for d in range(2):
        expand = jnp.where(row_j == (lane_j if d == 0 else L - 1 - lane_j), 1.0, 0.0).astype(BF16)
        select = jnp.where(sel_j == (L - 1 - sel_r if d == 0 else sel_r), 1.0, 0.0).astype(BF16)
        lr_c, li_c, dt_c = lamr_col[d, 0], lami_col[d, 0], jnp.exp(ldt_col[d, 0])
        pmag = jnp.exp(jcol * (lr_c * dt_c))
        pr0 = _exact_rhs_dot(pmag * jnp.cos(jcol * (li_c * dt_c)), expand)
        pi0 = _exact_rhs_dot(pmag * jnp.sin(jcol * (li_c * dt_c)), expand)
        mag_c = jnp.exp(lr_c * dt_c)
        ar_c, ai_c = mag_c * jnp.cos(li_c * dt_c), mag_c * jnp.sin(li_c * dt_c)
        pr1, pi1 = ar_c * pr0 - ai_c * pi0, ar_c * pi0 + ai_c * pr0
        cr = jnp.concatenate([crt_ref[d, 0]] * (K // LANES), axis=1)
        ci = jnp.concatenate([cit_ref[d, 0]] * (K // LANES), axis=1)
        lr_r, li_r, dt_r = lamr_row[d, 0], lami_row[d, 0], jnp.exp(ldt_row[d, 0])
        mag_r = jnp.exp(lr_r * dt_r)
        ar_r, ai_r = mag_r * jnp.cos(li_r * dt_r), mag_r * jnp.sin(li_r * dt_r)
        den = lr_r * lr_r + li_r * li_r
        fr = ((ar_r - 1.0) * lr_r + ai_r * li_r) / den
        fi = (ai_r * lr_r - (ar_r - 1.0) * li_r) / den
        bbr = fr * brt_ref[d, 0] - fi * bit_ref[d, 0]
        bbi = fr * bit_ref[d, 0] + fi * brt_ref[d, 0]
        kcat.append(_dot2(bbr, pr0 * cr - pi0 * ci) - _dot2(bbi, pr0 * ci + pi0 * cr))
        q_parts.append((pr1 * cr - pi1 * ci, -(pi1 * cr + pr1 * ci)))
        rmag = jnp.exp(jrow * (lr_r * dt_r))
        prr = rmag * jnp.cos(jrow * (li_r * dt_r))
        pir = rmag * jnp.sin(jrow * (li_r * dt_r))
        prs, pis = _exact_lhs_dot(select, prr), _exact_lhs_dot(select, pir)
        bbr_t = jnp.concatenate([bbr] * L, axis=0)
        bbi_t = jnp.concatenate([bbi] * L, axis=0)
        p_parts.append((prs * bbr_t - pis * bbi_t, prs * bbi_t + pis * bbr_t))
        al_parts.append((ar_r * prr[L - 1:L] - ai_r * pir[L - 1:L], ar_r * pir[L - 1:L] + ai_r * prr[L - 1:L]))
    zeros = jnp.zeros((P, K), F32)
    ksum = (jnp.concatenate([zeros, kcat[0]], axis=1)
            + pltpu.roll(jnp.concatenate([kcat[1], zeros], axis=1), P, 1))
    per_tile = LANES // P
    for b in range(per_tile):
        rolled = ksum if b == 0 else pltpu.roll(ksum, P * b, 1)
        for a in range(L // per_tile):
            r = per_tile * a + b
            t_ref[0, P * r:P * (r + 1), :] = rolled[:, K - LANES * a:2 * K - LANES * a].astype(BF16)
    p_ref[0] = jnp.concatenate([p_parts[0][0], p_parts[1][0], p_parts[0][1], p_parts[1][1]], axis=1).astype(BF16)
    q_ref[0] = jnp.concatenate([q_parts[0][0], q_parts[1][0], q_parts[0][1], q_parts[1][1]], axis=0).astype(BF16)
    al_ref[0] = jnp.concatenate([jnp.concatenate([al_parts[0][0], al_parts[1][0]], axis=1),
                                 jnp.concatenate([al_parts[0][1], al_parts[1][1]], axis=1)], axis=0)


def _ssm_operators(lam_re, lam_im, log_dt, b_re, b_im, c_re, c_im):
    L, G, N, P = SSM_CHUNK, SSM_GROUPS, SSM_STATE, SSM_GROUP
    K = L * P
    rowf = lambda v: v.reshape(2, G, 1, N)
    colf = lambda v: v.reshape(2, G, N, 1)
    ldt = jnp.broadcast_to(log_dt[:, :, None], (2, G, N))
    tr = lambda v: jnp.swapaxes(v, 2, 3)
    ct = lambda v: jnp.tile(tr(v), (1, 1, 1, LANES // P))
    spec = lambda shape: pl.BlockSpec((2, 1) + shape, lambda g: (0, g, 0, 0))
    t_sum, p_cat, q_cat, al = pl.pallas_call(
        _ssm_ops_kernel, grid=(G,),
        in_specs=[spec((1, N))] * 3 + [spec((N, 1))] * 3 + [spec((P, N))] * 2 + [spec((N, LANES))] * 2,
        out_specs=[pl.BlockSpec((1, K, K), lambda g: (g, 0, 0)), pl.BlockSpec((1, K, 4 * N), lambda g: (g, 0, 0)),
                   pl.BlockSpec((1, 4 * N, K), lambda g: (g, 0, 0)), pl.BlockSpec((1, 2, 2 * N), lambda g: (g, 0, 0))],
        out_shape=[jax.ShapeDtypeStruct((G, K, K), BF16), jax.ShapeDtypeStruct((G, K, 4 * N), BF16),
                   jax.ShapeDtypeStruct((G, 4 * N, K), BF16), jax.ShapeDtypeStruct((G, 2, 2 * N), F32)],
        compiler_params=_params(("parallel",)), name="ssm_ops",
    )(rowf(lam_re), rowf(lam_im), rowf(ldt), colf(lam_re), colf(lam_im), colf(ldt),
      tr(b_re), tr(b_im), ct(c_re), ct(c_im))
    return t_sum, p_cat, q_cat, al[:, 0, :].reshape(1, G * 2 * N), al[:, 1, :].reshape(1, G * 2 * N)


def _ssm_state_in_kernel(u_ref, p_ref, xre_ref, xim_ref):
    x = _dot(u_ref[0], p_ref[0])
    xre_ref[...] = x[:, 0:LANES]
    xim_ref[...] = x[:, LANES:2 * LANES]


def _ssm_state_in(u_t, p_cat):
    G, M, K = u_t.shape
    tm = min(M, 256)
    return pl.pallas_call(
        _ssm_state_in_kernel, grid=(G, M // tm),
        in_specs=[pl.BlockSpec((1, tm, K), lambda g, m: (g, m, 0)),
                  pl.BlockSpec((1, K, 2 * LANES), lambda g, m: (g, 0, 0))],
        out_specs=[pl.BlockSpec((tm, LANES), lambda g, m: (m, g))] * 2,
        out_shape=[jax.ShapeDtypeStruct((M, G * LANES), F32)] * 2,
        compiler_params=_params(("parallel", "parallel")), name="ssm_state_in",
    )(u_t, p_cat)


def _ssm_scan_kernel(xre_ref, xim_ref, are_ref, aim_ref, hfre_ref, hfim_ref, hbre_ref, hbim_ref):
    nc = xre_ref.shape[1]
    width = xre_ref.shape[2]
    ar = are_ref[...]
    ai = aim_ref[...]
    lane = lax.broadcasted_iota(jnp.int32, (1, width), 1)
    is_fwd = (lane % LANES) < SSM_STATE

    def step(i, carry):
        hr, hi = carry
        ib = nc - 1 - i
        hfre_ref[0, pl.ds(i, 1), :] = hr
        hfim_ref[0, pl.ds(i, 1), :] = hi
        hbre_ref[0, pl.ds(ib, 1), :] = hr
        hbim_ref[0, pl.ds(ib, 1), :] = hi
        xr = jnp.where(is_fwd, xre_ref[0, pl.ds(i, 1), :], xre_ref[0, pl.ds(ib, 1), :])
        xi = jnp.where(is_fwd, xim_ref[0, pl.ds(i, 1), :], xim_ref[0, pl.ds(ib, 1), :])
        return ar * hr - ai * hi + xr, ar * hi + ai * hr + xi

    zero = jnp.zeros((1, width), F32)
    lax.fori_loop(0, nc, step, (zero, zero))


def _ssm_scan(xre, xim, al_re, al_im):
    B, nc, width = xre.shape
    blk = pl.BlockSpec((1, nc, width), lambda b: (b, 0, 0))
    vec = pl.BlockSpec((1, width), lambda b: (0, 0))
    return pl.pallas_call(
        _ssm_scan_kernel, grid=(B,),
        in_specs=[blk, blk, vec, vec], out_specs=[blk] * 4,
        out_shape=[jax.ShapeDtypeStruct((B, nc, width), F32)] * 4,
        compiler_params=_params(("parallel",)), name="ssm_scan",
    )(xre, xim, al_re, al_im)


def _ssm_out_kernel(u_ref, t_ref, hfre_ref, hfim_ref, hbre_ref, hbim_ref, q_ref, y_ref):
    tm = u_ref.shape[1]
    lane = lax.broadcasted_iota(jnp.int32, (tm, LANES), 1)
    is_fwd = lane < SSM_STATE
    h_re = jnp.where(is_fwd, hfre_ref[...], hbre_ref[...])
    h_im = jnp.where(is_fwd, hfim_ref[...], hbim_ref[...])
    h_cat = jnp.concatenate([h_re, h_im], axis=1).astype(BF16)
    y_ref[0] = _dot(u_ref[0], t_ref[0]) + _dot(h_cat, q_ref[0])


def _ssm_out(u_t, t_sum, hs, q_cat):
    G, M, K = u_t.shape
    tm = min(M, 256)
    hblk = pl.BlockSpec((tm, LANES), lambda g, m: (m, g))
    return pl.pallas_call(
        _ssm_out_kernel, grid=(G, M // tm),
        in_specs=[pl.BlockSpec((1, tm, K), lambda g, m: (g, m, 0)),
                  pl.BlockSpec((1, K, K), lambda g, m: (g, 0, 0)),
                  hblk, hblk, hblk, hblk,
                  pl.BlockSpec((1, 2 * LANES, K), lambda g, m: (g, 0, 0))],
        out_specs=pl.BlockSpec((1, tm, K), lambda g, m: (g, m, 0)),
        out_shape=jax.ShapeDtypeStruct((G, M, K), F32),
        compiler_params=_params(("parallel", "parallel")), name="ssm_out",
    )(u_t, t_sum, *hs, q_cat)


def _s5_scan(u, ops):
    t_sum, p_cat, q_cat, al_re, al_im = ops
    B, S, _ = u.shape
    L, G, P = SSM_CHUNK, SSM_GROUPS, SSM_GROUP
    nc = S // L
    u_t = u.reshape(B, nc, L, G, P).transpose(3, 0, 1, 2, 4).reshape(G, B * nc, L * P).astype(BF16)
    xre, xim = _ssm_state_in(u_t, p_cat)
    width = G * LANES
    hs = _ssm_scan(xre.reshape(B, nc, width), xim.reshape(B, nc, width), al_re, al_im)
    hfre, hfim, hbre, hbim = (h.reshape(B * nc, width) for h in hs)
    y = _ssm_out(u_t, t_sum, (hfre, hfim, hbre, hbim), q_cat)
    return y.reshape(G, B, nc, L, P).transpose(1, 2, 3, 0, 4).reshape(B, S, G * P)


def _na_bias_kernel(rpb_ref, out_ref):
    h = pl.program_id(0)
    n_dr, n_dc = 2 * NA_WIN_R - 1, 2 * NA_WIN_C - 1
    qc = lax.broadcasted_iota(jnp.int32, (GRID_W, GRID_W), 0)
    kc = lax.broadcasted_iota(jnp.int32, (GRID_W, GRID_W), 1)
    dc = jnp.clip(kc - qc + (NA_WIN_C - 1), 0, n_dc - 1)
    cstart = jnp.clip(qc - NA_WIN_C // 2, 0, GRID_W - NA_WIN_C)
    col_ok = (kc >= cstart) & (kc < cstart + NA_WIN_C)
    base = []
    for dr in range(n_dr):
        acc = jnp.zeros((GRID_W, GRID_W), F32)
        for d in range(n_dc):
            acc = jnp.where(dc == d, rpb_ref[(h * n_dr + dr) * n_dc + d], acc)
        base.append(jnp.where(col_ok, acc, NEG))
    for off in range(NA_WIN_R):
        out_ref[0, off] = jnp.concatenate([base[w - off + NA_WIN_R - 1] for w in range(NA_WIN_R)], axis=1)


def _na_bias_table(rpb):
    return pl.pallas_call(
        _na_bias_kernel, grid=(HEADS,),
        in_specs=[pl.BlockSpec(memory_space=pltpu.SMEM)],
        out_specs=pl.BlockSpec((1, NA_WIN_R, GRID_W, NA_WIN_R * GRID_W), lambda h: (h, 0, 0, 0)),
        out_shape=jax.ShapeDtypeStruct((HEADS, NA_WIN_R, GRID_W, NA_WIN_R * GRID_W), F32),
        compiler_params=_params(("parallel",)), name="na_bias",
    )(rpb.astype(F32).reshape(-1))


def _na_kernel(q_ref, kp_ref, kc_ref, kn_ref, vp_ref, vc_ref, vn_ref, bias_ref, o_ref, kbuf, vbuf, *, rows):
    rb = pl.program_id(1)
    Wg = GRID_W
    C = HEAD_DIM
    halo = NA_HALO_ROWS * Wg
    cur = NA_ROWS_PER_STEP * Wg
    win = NA_WIN_R * Wg
    kbuf[0:halo, :] = kp_ref[0]
    kbuf[halo:halo + cur, :] = kc_ref[0]
    kbuf[halo + cur:2 * halo + cur, :] = kn_ref[0]
    vbuf[0:halo, :] = vp_ref[0]
    vbuf[halo:halo + cur, :] = vc_ref[0]
    vbuf[halo + cur:2 * halo + cur, :] = vn_ref[0]
    r0 = rb * NA_ROWS_PER_STEP
    group = 2
    for j0 in range(0, NA_ROWS_PER_STEP, group):
        scores, values = [], []
        for j in range(j0, j0 + group):
            r = r0 + j
            rs = jnp.clip(r - NA_WIN_R // 2, 0, rows - NA_WIN_R)
            off = r - rs
            start = pl.multiple_of((rs - r0 + NA_HALO_ROWS) * Wg, Wg)
            kw = kbuf[pl.ds(start, win), :]
            values.append(vbuf[pl.ds(start, win), :])
            for h in range(HEADS):
                sl = slice(h * C, (h + 1) * C)
                scores.append(_dot_nt(q_ref[0, j * Wg:(j + 1) * Wg, sl], kw[:, sl]) + bias_ref[h, off])
        s = jnp.concatenate(scores, axis=0)
        p = jnp.exp(s - jnp.max(s, axis=-1, keepdims=True))
        inv_l = 1.0 / jnp.sum(p, axis=-1, keepdims=True)
        pb = p.astype(BF16)
        for g in range(group):
            outs = []
            for h in range(HEADS):
                blk = slice((g * HEADS + h) * Wg, (g * HEADS + h + 1) * Wg)
                outs.append(_dot(pb[blk], values[g][:, h * C:(h + 1) * C]) * inv_l[blk])
            o_ref[0, (j0 + g) * Wg:(j0 + g + 1) * Wg, :] = jnp.concatenate(outs, axis=1)


def _neighbourhood_attention(q, k, v, bias):
    B, S, W = q.shape
    rows = S // GRID_W
    cur = NA_ROWS_PER_STEP * GRID_W
    halo = NA_HALO_ROWS * GRID_W
    nrb = S // cur
    per = cur // halo
    nh = S // halo
    blk = pl.BlockSpec((1, cur, W), lambda b, i: (b, i, 0))
    prev = pl.BlockSpec((1, halo, W), lambda b, i: (b, jnp.maximum(i * per - 1, 0), 0))
    nxt = pl.BlockSpec((1, halo, W), lambda b, i: (b, jnp.minimum((i + 1) * per, nh - 1), 0))
    bspec = pl.BlockSpec(bias.shape, lambda b, i: (0, 0, 0, 0))
    return pl.pallas_call(
        functools.partial(_na_kernel, rows=rows), grid=(B, nrb),
        in_specs=[blk, prev, blk, nxt, prev, blk, nxt, bspec],
        out_specs=blk, out_shape=jax.ShapeDtypeStruct((B, S, W), F32),
        scratch_shapes=[pltpu.VMEM((cur + 2 * halo, W), BF16), pltpu.VMEM((cur + 2 * halo, W), BF16)],
        compiler_params=_params(("parallel", "parallel")), name="neighbourhood_attn",
    )(q, k, k, k, v, v, v, bias)


def _post_kernel(x_ref, of_ref, ob_ref, gate_ref, yb_ref, ys_ref, u_ref, yd_ref,
                 dnw_ref, bd_ref, dskip_ref, gluw_ref, glub_ref, wout_ref, ln2_ref, w1_ref, w2_ref, fnw_ref,
                 out_ref, *, final):
    x = x_ref[...]
    o = of_ref[...] + ob_ref[...]
    ms = _head_sums(o * o, bd_ref[...]) * (1.0 / HEAD_DIM)
    g = gate_ref[...]
    y_a = o * lax.rsqrt(ms + EPS) * dnw_ref[...] * (g * _sigmoid(g))
    y = ys_ref[...] + dskip_ref[...] * u_ref[...]
    z = 0.5 * y * (1.0 + jnp.tanh(math.sqrt(2.0 / math.pi) * (y + 0.044715 * (y * y * y))))
    y_c = z * _sigmoid(_dot(z.astype(BF16), gluw_ref[...]) + glub_ref[...])
    mix = jnp.concatenate([y_a, yb_ref[0], yb_ref[1], y_c, yd_ref[...]], axis=1).astype(BF16)
    x1 = x + _dot(mix, wout_ref[...])
    hb = _rms(x1, ln2_ref[...]).astype(BF16)
    ffc = D_MODEL
    acc = x1
    for c in range(D_FF // ffc):
        a = jnp.maximum(_dot(hb, w1_ref[:, c * ffc:(c + 1) * ffc]), 0.0)
        acc = acc + _dot((a * a).astype(BF16), w2_ref[c * ffc:(c + 1) * ffc, :])
    if final:
        acc = _rms(acc, fnw_ref[...])
    out_ref[...] = acc


def _post(x, o_f, o_b, gate, y_b, y_s, u, y_d, wts, final):
    T = x.shape[0]
    tm = min(TOKEN_TILE, T)
    W = WIDTH
    row = lambda n: pl.BlockSpec((tm, n), lambda i: (i, 0))
    const = lambda a: pl.BlockSpec(a.shape, lambda i: (0, 0), pipeline_mode=pl.Buffered(1))
    return pl.pallas_call(
        functools.partial(_post_kernel, final=final), grid=(T // tm,),
        in_specs=([row(D_MODEL)] + [row(W)] * 3 + [pl.BlockSpec((2, tm, LANES), lambda i: (0, i, 0))] + [row(W)] * 3
                  + [const(a) for a in wts]),
        out_specs=row(D_MODEL), out_shape=jax.ShapeDtypeStruct((T, D_MODEL), F32),
        compiler_params=_params(("parallel",)), name="post",
    )(x, o_f, o_b, gate, y_b, y_s, u, y_d, *wts)


def _rope_tables(seq):
    inv_freq = ROPE_THETA ** (-jnp.arange(0, HEAD_DIM, 2, dtype=F32) / HEAD_DIM)
    ang = jnp.arange(seq, dtype=F32)[:, None] * inv_freq[None, :]
    cos, sin = jnp.cos(ang), jnp.sin(ang)
    cos_t = jnp.tile(jnp.concatenate([cos, cos], axis=1), (1, HEADS))
    sin_t = jnp.tile(jnp.concatenate([-sin, sin], axis=1), (1, HEADS))
    return cos_t, sin_t


def _layer_weights(i, p):
    W = WIDTH
    w = p['w_in'][i]
    o_gate, o_a, o_b, o_dil, o_ssm, o_na = 3 * W, 4 * W, 4 * W + 2 * HEADS, 4 * W + 4 * HEADS, 7 * W + 4 * HEADS, 8 * W + 4 * HEADS
    pad = jnp.zeros((D_MODEL, LANES - 4 * HEADS), F32)
    w_in = jnp.concatenate([w[:, 0:o_a], w[:, o_dil:o_na + 3 * W], w[:, o_a:o_dil], pad], axis=1).astype(BF16)
    assert w_in.shape[1] == IN_COLS_PADDED
    lane_pad = lambda v: jnp.concatenate([v.reshape(1, -1).astype(F32), jnp.zeros((1, LANES - v.size), F32)], axis=1)
    idx = np.arange(W)
    bd = jnp.asarray((idx[:, None] // HEAD_DIM) == (idx[None, :] // HEAD_DIM), BF16)
    row = lambda v: v.reshape(1, -1).astype(F32)
    return dict(
        ln1=row(p['ln1_w'][i]), w_in=w_in,
        conv_w=p['dn_conv_w'][i].astype(F32), alog=lane_pad(p['dn_a_log'][i]), dtb=lane_pad(p['dn_dt_bias'][i]),
        bd=bd,
        ssm=_ssm_operators(*(p[n][i].astype(F32) for n in
                             ('ssm_lam_re', 'ssm_lam_im', 'ssm_log_dt', 'ssm_b_re', 'ssm_b_im', 'ssm_c_re', 'ssm_c_im'))),
        na_bias=_na_bias_table(p['na_rpb'][i]),
        post=(row(jnp.tile(p['dn_norm_w'][i], HEADS)), bd, row(p['ssm_d'][i]), p['ssm_glu_w'][i].astype(BF16),
              row(p['ssm_glu_b'][i]), p['w_out'][i].astype(BF16), row(p['ln2_w'][i]),
              p['w_ff1'][i].astype(BF16), p['w_ff2'][i].astype(BF16), row(p['final_norm_w'])),
    )


def _trunk(x, layers):
    B, S, D = x.shape
    T = B * S
    cos_t, sin_t = _rope_tables(S)
    xt = x.reshape(T, D)
    seq3 = lambda a: a.reshape(B, S, a.shape[-1])
    flat = lambda a: a.reshape(T, a.shape[-1])
    for i, lw in enumerate(layers):
        z_dn, z_gate, z_ab, dq, dk, dv, z_ssm, nq, nk, nv = _in_proj(xt, lw['ln1'], lw['w_in'], cos_t, sin_t, S)
        q, k, v, gb = _dn_prep(seq3(z_dn), seq3(z_ab), lw['conv_w'], lw['alog'], lw['dtb'], lw['bd'])
        o_f, o_b = _dn_scan(q, k, v, gb)
        pairs = lambda a: a.reshape(2, B, S, LANES)
        y_b = _dilated_attention(pairs(dq), pairs(dk), pairs(dv)).reshape(2, T, LANES)
        y_s = _s5_scan(seq3(z_ssm), lw['ssm'])
        y_d = _neighbourhood_attention(seq3(nq), seq3(nk), seq3(nv), lw['na_bias'])
        xt = _post(xt, flat(o_f), flat(o_b), z_gate, y_b, flat(y_s), z_ssm, flat(y_d),
                   lw['post'], final=(i == len(layers) - 1))
    return xt.reshape(B, S, D)


def kernel(x_prompt, x_sample, ln1_w, w_in, dn_conv_w, dn_a_log, dn_dt_bias, dn_norm_w, ssm_lam_re, ssm_lam_im,
           ssm_log_dt, ssm_b_re, ssm_b_im, ssm_c_re, ssm_c_im, ssm_d, ssm_glu_w, ssm_glu_b, na_rpb, w_out,
           ln2_w, w_ff1, w_ff2, final_norm_w):
    p = dict(ln1_w=ln1_w, w_in=w_in, dn_conv_w=dn_conv_w, dn_a_log=dn_a_log, dn_dt_bias=dn_dt_bias,
             dn_norm_w=dn_norm_w, ssm_lam_re=ssm_lam_re, ssm_lam_im=ssm_lam_im, ssm_log_dt=ssm_log_dt,
             ssm_b_re=ssm_b_re, ssm_b_im=ssm_b_im, ssm_c_re=ssm_c_re, ssm_c_im=ssm_c_im, ssm_d=ssm_d,
             ssm_glu_w=ssm_glu_w, ssm_glu_b=ssm_glu_b, na_rpb=na_rpb, w_out=w_out, ln2_w=ln2_w,
             w_ff1=w_ff1, w_ff2=w_ff2, final_norm_w=final_norm_w)
    layers = [_layer_weights(i, p) for i in range(DEPTH)]
    return (_trunk(x_prompt, layers), _trunk(x_sample, layers))
```

```python
import functools
import math

import numpy as np
import jax
import jax.numpy as jnp
from jax import lax
from jax.experimental import pallas as pl
from jax.experimental.pallas import tpu as pltpu

F32 = jnp.float32
BF16 = jnp.bfloat16

D_MODEL = 1024
DEPTH = 2
HEAD_DIM = 64
HEADS = 4
WIDTH = HEADS * HEAD_DIM
DN_CHUNK = 64
CONV_WIDTH = 5
DIL_PATTERNS = ((128, 1), (512, 4), (2048, 16))
SSM_GROUP = 16
SSM_GROUPS = 16
SSM_STATE = 64
SSM_CHUNK = 64
GRID_W = 64
NA_WIN_R = 8
NA_WIN_C = 16
D_FF = 4 * D_MODEL
ROPE_THETA = 10000.0
EPS = 1e-6
NEG = -1e30

LANES = 128
TOKEN_TILE = 512
DIL_TILE = 1024
DIL_BAND = 64
NA_ROWS_PER_STEP = 8
NA_HALO_ROWS = 4
IN_COLS_PADDED = 23 * LANES
VMEM_LIMIT = 56 * 2**20


def _params(sem):
    return pltpu.CompilerParams(dimension_semantics=sem, vmem_limit_bytes=VMEM_LIMIT)


def _dot(a, b):
    return jnp.dot(a, b, preferred_element_type=F32)


def _dot_nt(a, b):
    return lax.dot_general(a, b, (((1,), (1,)), ((), ())), preferred_element_type=F32)


def _bmm(a, b):
    return lax.dot_general(a, b, (((2,), (1,)), ((0,), (0,))), preferred_element_type=F32)


def _bmm_nt(a, b):
    return lax.dot_general(a, b, (((2,), (2,)), ((0,), (0,))), preferred_element_type=F32)


def _split2(x):
    hi = x.astype(BF16)
    lo = (x - hi.astype(F32)).astype(BF16)
    return hi, lo


def _split3(x):
    hi = x.astype(BF16)
    r = x - hi.astype(F32)
    mid = r.astype(BF16)
    lo = (r - mid.astype(F32)).astype(BF16)
    return hi, mid, lo


def _exact_lhs_dot(m, x):
    hi, mid, lo = _split3(x)
    return _dot(m, hi) + _dot(m, mid) + _dot(m, lo)


def _head_sums(sq, bd):
    hi, lo = _split2(sq)
    return _dot(hi, bd) + _dot(lo, bd)


def _sigmoid(x):
    return 1.0 / (1.0 + jnp.exp(-x))


def _rms(x, w):
    ms = jnp.mean(x * x, axis=-1, keepdims=True)
    return x * lax.rsqrt(ms + EPS) * w


def _in_proj_kernel(x_ref, xp_ref, xn_ref, lnw_ref, w_ref, cos_ref, sin_ref, cw_ref, alog_ref, dtb_ref, bd_ref,
                    q_ref, k_ref, v_ref, gb_ref, gate_ref, dq_ref, dk_ref, dv_ref, ssm_ref, nq_ref, nk_ref, nv_ref,
                    buf, *, per_seq):
    lnw = lnw_ref[...]
    z = _dot(_rms(x_ref[...], lnw).astype(BF16), w_ref[...])
    tm = z.shape[0]
    W = WIDTH
    gate_ref[...] = z[:, 3 * W:4 * W]
    cos = cos_ref[...]
    sin = sin_ref[...]
    lane = lax.broadcasted_iota(jnp.int32, (tm, W), 1)
    first_half = (lane % HEAD_DIM) < (HEAD_DIM // 2)

    def rope(t):
        rot = jnp.where(first_half, pltpu.roll(t, W - HEAD_DIM // 2, 1), pltpu.roll(t, HEAD_DIM // 2, 1))
        return t * cos + rot * sin

    scale = HEAD_DIM ** -0.5
    for ref, t in ((dq_ref, rope(z[:, 4 * W:5 * W]) * scale), (dk_ref, rope(z[:, 5 * W:6 * W])),
                   (dv_ref, z[:, 6 * W:7 * W])):
        ref[0] = t[:, 0:LANES]
        ref[1] = t[:, LANES:2 * LANES]
    ssm_ref[...] = z[:, 7 * W:8 * W]
    nq_ref[...] = (z[:, 8 * W:9 * W] * scale).astype(BF16)
    nk_ref[...] = z[:, 9 * W:10 * W].astype(BF16)
    nv_ref[...] = z[:, 10 * W:11 * W].astype(BF16)

    j = pl.program_id(0) % per_seq
    half = CONV_WIDTH // 2
    w_dn = w_ref[:, 0:3 * W]
    zp = _dot(_rms(xp_ref[...], lnw).astype(BF16), w_dn)
    zn = _dot(_rms(xn_ref[...], lnw).astype(BF16), w_dn)
    buf[0:8, :] = jnp.where(j > 0, zp, 0.0)
    buf[8:8 + tm, :] = z[:, 0:3 * W]
    buf[8 + tm:16 + tm, :] = jnp.where(j < per_seq - 1, zn, 0.0)
    cw = cw_ref[...]
    acc = buf[8 - half:8 - half + tm, :] * cw[0:1, :]
    for t in range(1, CONV_WIDTH):
        acc = acc + buf[8 - half + t:8 - half + t + tm, :] * cw[t:t + 1, :]
    y = acc * _sigmoid(acc)
    bd = bd_ref[...]
    q = y[:, 0:W]
    k = y[:, W:2 * W]
    q_ref[...] = q * lax.rsqrt(_head_sums(q * q, bd) + EPS) * scale
    k_ref[...] = k * lax.rsqrt(_head_sums(k * k, bd) + EPS)
    v_ref[...] = y[:, 2 * W:3 * W]
    ab = z[:, 11 * W:11 * W + LANES]
    t = ab + dtb_ref[...]
    softplus = jnp.maximum(t, 0.0) + jnp.log(1.0 + jnp.exp(-jnp.abs(t)))
    g = -jnp.exp(alog_ref[...]) * softplus
    lane_ab = lax.broadcasted_iota(jnp.int32, ab.shape, 1)
    gb_ref[...] = jnp.where(lane_ab < 2 * HEADS, g, _sigmoid(ab))


def _in_proj(x, lnw, w, cos_t, sin_t, conv_w, alog, dtb, bd, seq):
    T = x.shape[0]
    tm = min(TOKEN_TILE, seq)
    per_seq = seq // tm
    r8 = tm // 8
    W = WIDTH
    row = lambda n: pl.BlockSpec((tm, n), lambda i: (i, 0))
    const = lambda a: pl.BlockSpec(a.shape, lambda i: (0, 0))
    tab = pl.BlockSpec((tm, W), lambda i: (i % per_seq, 0))
    prev = pl.BlockSpec((8, D_MODEL), lambda i: (jnp.maximum(i * r8 - 1, 0), 0))
    nxt = pl.BlockSpec((8, D_MODEL), lambda i: (jnp.minimum((i + 1) * r8, T // 8 - 1), 0))
    pair = pl.BlockSpec((2, tm, LANES), lambda i: (0, i, 0))
    pair_shape = jax.ShapeDtypeStruct((2, T, LANES), F32)
    f32 = lambda n: jax.ShapeDtypeStruct((T, n), F32)
    bf16 = lambda n: jax.ShapeDtypeStruct((T, n), BF16)
    out_shapes = [f32(W), f32(W), f32(W), f32(LANES), f32(W), pair_shape, pair_shape, pair_shape, f32(W),
                  bf16(W), bf16(W), bf16(W)]
    out_specs = [row(W), row(W), row(W), row(LANES), row(W), pair, pair, pair, row(W), row(W), row(W), row(W)]
    return pl.pallas_call(
        functools.partial(_in_proj_kernel, per_seq=per_seq), grid=(T // tm,),
        in_specs=[row(D_MODEL), prev, nxt, const(lnw), const(w), tab, tab,
                  const(conv_w), const(alog), const(dtb), const(bd)],
        out_specs=out_specs, out_shape=out_shapes,
        scratch_shapes=[pltpu.VMEM((tm + 16, 3 * W), F32)],
        compiler_params=_params(("parallel",)), name="in_proj",
    )(x, x, x, lnw, w, cos_t, sin_t, conv_w, alog, dtb, bd)


def _dn_scan_kernel(qf_ref, kf_ref, vf_ref, gf_ref, qr_ref, kr_ref, vr_ref, gr_ref, of_ref, or_ref,
                    s_scr, u_scr, w_scr, qg_scr, at_scr, kdt_scr, *, nc):
    C = DN_CHUNK
    tb = nc * C
    H = HEADS
    assert LANES == 2 * C

    @pl.when(pl.program_id(1) == 0)
    def _():
        s_scr[...] = jnp.zeros_like(s_scr)

    r = lax.broadcasted_iota(jnp.int32, (tb, tb), 0)
    c = lax.broadcasted_iota(jnp.int32, (tb, tb), 1)
    same = (r // C) == (c // C)
    ri = lax.broadcasted_iota(jnp.int32, (LANES, LANES), 0)
    ci = lax.broadcasted_iota(jnp.int32, (LANES, LANES), 1)
    eye128 = jnp.where(ri == ci, 1.0, 0.0).astype(BF16)
    same_head = (ri < C) == (ci < C)
    lane = lax.broadcasted_iota(jnp.int32, (tb, LANES), 1)
    first = lane < C
    ii = lax.broadcasted_iota(jnp.int32, (tb, LANES), 0) % C
    jj = lane % C
    lane_c = lax.broadcasted_iota(jnp.int32, (1, LANES), 1)

    def block_diag(x):
        ln = lax.broadcasted_iota(jnp.int32, x.shape, x.ndim - 1)
        zero = jnp.zeros_like(x)
        return jnp.concatenate([jnp.where(ln < C, x, zero), jnp.where(ln >= C, x, zero)], axis=x.ndim - 2)

    pick = lambda a, col: jnp.where(first, a[:, col:col + 1], a[:, col + 1:col + 2])
    parts = {name: [] for name in ('q', 'k', 'v', 'gam', 'gtot', 'beta', 'grow', 'incl', 'strict')}
    gtots = []
    for d, (q_ref, k_ref, v_ref, gb_ref) in enumerate(((qf_ref, kf_ref, vf_ref, gf_ref),
                                                       (qr_ref, kr_ref, vr_ref, gr_ref))):
        rev = d == 1
        gb = gb_ref[0]
        cum = same & ((c >= r) if rev else (c <= r))
        gam = _exact_lhs_dot(jnp.where(cum, 1.0, 0.0).astype(BF16), gb)
        gam3 = gam.reshape(nc, C, LANES)
        last = gam3[:, 0:1, :] if rev else gam3[:, C - 1:C, :]
        gtot = jnp.broadcast_to(last, (nc, C, LANES)).reshape(tb, LANES)
        gtots.append(gtot)
        g_hi, g_mid, g_lo = _split3(gam)
        gam_t = _dot_nt(eye128, g_hi) + _dot_nt(eye128, g_mid) + _dot_nt(eye128, g_lo)
        for pr in range(H // 2):
            ch0 = d * H + 2 * pr
            tile = slice(pr * LANES, (pr + 1) * LANES)
            parts['q'].append(q_ref[0, :, tile])
            parts['k'].append(k_ref[0, :, tile])
            parts['v'].append(v_ref[0, :, tile])
            parts['gam'].append(pick(gam, ch0))
            parts['gtot'].append(pick(gtot, ch0))
            parts['beta'].append(pick(gb, 2 * H + ch0))
            parts['grow'].append(jnp.concatenate(
                [jnp.concatenate([jnp.broadcast_to(gam_t[ch0:ch0 + 1, cc * C:(cc + 1) * C], (C, C)),
                                  jnp.broadcast_to(gam_t[ch0 + 1:ch0 + 2, cc * C:(cc + 1) * C], (C, C))], axis=1)
                 for cc in range(nc)], axis=0))
            parts['incl'].append((jj >= ii) if rev else (jj <= ii))
            parts['strict'].append((jj > ii) if rev else (jj < ii))
    stk = {name: jnp.concatenate(vals, axis=0) for name, vals in parts.items()}
    slots = 2 * (H // 2)
    rows_all = slots * tb
    chunks = lambda x: x.astype(BF16).reshape(slots * nc, C, x.shape[-1])
    ii_all = jnp.concatenate([ii] * slots, axis=0)
    jj_all = jnp.concatenate([jj] * slots, axis=0)
    qp, kp, vp, gam_c, beta = stk['q'], stk['k'], stk['v'], stk['gam'], stk['beta']
    dec = jnp.exp(jnp.where(stk['incl'], gam_c - stk['grow'], NEG))
    kb = kp * beta
    k_bd = block_diag(chunks(kp))
    a_raw = _bmm_nt(chunks(kb), k_bd).reshape(rows_all, LANES)
    qk = _bmm_nt(chunks(qp), k_bd).reshape(rows_all, LANES)
    lm = jnp.where(stk['strict'], a_raw * dec, 0.0)
    attn = qk * dec
    t_inv = jnp.where(ii_all == jj_all, 1.0, 0.0) - jnp.where((ii_all // 2) == (jj_all // 2), lm, 0.0)
    n = 2
    while n < C:
        off = ((ii_all // (2 * n)) == (jj_all // (2 * n))) & ((ii_all // n) != (jj_all // n))
        t3 = chunks(t_inv)
        bt = _bmm(chunks(jnp.where(off, lm, 0.0)), block_diag(t3))
        t_inv = t_inv - _bmm(t3, block_diag(bt.astype(BF16))).reshape(rows_all, LANES)
        n *= 2
    eg = jnp.exp(gam_c)
    rhs = jnp.concatenate([block_diag(chunks(vp * beta)), block_diag(chunks(kb * eg))], axis=2)
    sol = _bmm(chunks(t_inv), rhs).reshape(rows_all, 2 * LANES)
    u_scr[...] = sol[:, 0:LANES].reshape(slots, tb, LANES)
    w_scr[...] = sol[:, LANES:2 * LANES].astype(BF16).reshape(slots, tb, LANES)
    qg_scr[...] = (qp * eg).astype(BF16).reshape(slots, tb, LANES)
    at_scr[...] = attn.astype(BF16).reshape(slots, tb, LANES)
    kd = (kp * jnp.exp(stk['gtot'] - gam_c)).astype(BF16)
    for slot in range(slots):
        kdt_scr[slot] = _dot_nt(eye128, kd[slot * tb:(slot + 1) * tb]).astype(BF16)

    for step in range(nc):
        rng = []
        for slot in range(slots):
            cc = nc - 1 - step if slot >= H // 2 else step
            rng.append((cc * C, (cc + 1) * C))
        take = lambda ref: jnp.stack([ref[slot, lo:hi, :] for slot, (lo, hi) in enumerate(rng)], axis=0)
        s = s_scr[...]
        sb = s.astype(BF16)
        v_new = take(u_scr) - _bmm(take(w_scr), sb)
        vb = v_new.astype(BF16)
        o = _bmm(take(qg_scr), sb) + _bmm(take(at_scr), block_diag(vb))
        gl = []
        for slot, (lo, hi) in enumerate(rng):
            d, pr = divmod(slot, H // 2)
            ch0 = d * H + 2 * pr
            o_ref = or_ref if d == 1 else of_ref
            o_ref[0, lo:hi, pr * LANES:(pr + 1) * LANES] = o[slot]
            gt = gtots[d]
            gl.append(jnp.exp(jnp.where(lane_c < C, gt[lo:lo + 1, ch0:ch0 + 1], gt[lo:lo + 1, ch0 + 1:ch0 + 2])))
        kdt = jnp.stack([kdt_scr[slot, :, lo:hi] for slot, (lo, hi) in enumerate(rng)], axis=0)
        upd = _bmm(kdt, vb)
        s_scr[...] = s * jnp.stack(gl, axis=0) + jnp.where(same_head[None], upd, 0.0)


def _dn_scan(q, k, v, gb):
    B, S, W = q.shape
    tb = min(TOKEN_TILE, S)
    nb = S // tb
    nc = tb // DN_CHUNK
    fwd = lambda n: pl.BlockSpec((1, tb, n), lambda b, i: (b, i, 0))
    bwd = lambda n: pl.BlockSpec((1, tb, n), lambda b, i: (b, nb - 1 - i, 0))
    slots = HEADS
    return pl.pallas_call(
        functools.partial(_dn_scan_kernel, nc=nc), grid=(B, nb),
        in_specs=[fwd(W), fwd(W), fwd(W), fwd(LANES), bwd(W), bwd(W), bwd(W), bwd(LANES)],
        out_specs=[fwd(W), bwd(W)], out_shape=[jax.ShapeDtypeStruct((B, S, W), F32)] * 2,
        scratch_shapes=[pltpu.VMEM((slots, LANES, LANES), F32), pltpu.VMEM((slots, tb, LANES), F32),
                        pltpu.VMEM((slots, tb, LANES), BF16), pltpu.VMEM((slots, tb, LANES), BF16),
                        pltpu.VMEM((slots, tb, LANES), BF16), pltpu.VMEM((slots, LANES, tb), BF16)],
        compiler_params=_params(("parallel", "arbitrary")), name="dn_scan",
    )(q, k, v, gb, q, k, v, gb)


def _dil_kernel(q_ref, kp_ref, kc_ref, kn_ref, vp_ref, vc_ref, vn_ref, o_ref,
                kwin, vwin, m_st, l_st, acc_st, *, seq):
    i = pl.program_id(2)
    tq = DIL_TILE
    band = DIL_BAND
    C = HEAD_DIM
    assert LANES == 2 * C
    kwin[0:tq, :] = kp_ref[0, 0]
    kwin[tq:2 * tq, :] = kc_ref[0, 0]
    kwin[2 * tq:3 * tq, :] = kn_ref[0, 0]
    vwin[0:tq, :] = vp_ref[0, 0]
    vwin[tq:2 * tq, :] = vc_ref[0, 0]
    vwin[2 * tq:3 * tq, :] = vn_ref[0, 0]

    def rows(start, size, stride):
        return pl.ds(start, size) if stride == 1 else pl.ds(start, size, stride=stride)

    for pi, (window, dil) in enumerate(DIL_PATTERNS):
        assert window // (2 * dil) == band and band * dil <= tq
        qb = min(2 * band, tq // dil)
        nk = qb + 2 * band
        qa = lax.broadcasted_iota(jnp.int32, (qb, nk), 0)
        kj = lax.broadcasted_iota(jnp.int32, (qb, nk), 1)
        band_bias = jnp.where((kj >= qa) & (kj <= qa + 2 * band), 0.0, NEG)
        kcol = lax.broadcasted_iota(jnp.int32, (1, nk), 1)
        vlane = lax.broadcasted_iota(jnp.int32, (nk, LANES), 1)
        qlane = lax.broadcasted_iota(jnp.int32, (qb, LANES), 1)
        one = jnp.ones((nk, LANES), BF16)
        zero = jnp.zeros((nk, LANES), BF16)
        units = [(c, u) for c in range(dil) for u in range(tq // (dil * qb))]
        group = 4
        first = qlane < C
        for g0 in range(0, len(units), group):
            scores, values = [], []
            for c, u in units[g0:g0 + group]:
                q0 = c + dil * u * qb
                k0 = tq + q0 - dil * band
                pos = i * tq + (q0 - dil * band) + dil * kcol
                bias = band_bias + jnp.where((pos >= 0) & (pos < seq), 0.0, NEG)
                qr = q_ref[0, 0, rows(q0, qb, dil), :].astype(BF16)
                kr = kwin[rows(k0, nk, dil), :].astype(BF16)
                values.append(vwin[rows(k0, nk, dil), :].astype(BF16))
                if nk % LANES == 0:
                    k_bd = jnp.concatenate([jnp.where(vlane < C, kr, zero), jnp.where(vlane >= C, kr, zero)], axis=0)
                    s2 = _dot_nt(qr, k_bd)
                    scores.append(s2[:, 0:nk] + bias)
                    scores.append(s2[:, nk:2 * nk] + bias)
                else:
                    for hh in range(LANES // C):
                        sl = slice(hh * C, (hh + 1) * C)
                        scores.append(_dot_nt(qr[:, sl], kr[:, sl]) + bias)
            s = jnp.concatenate(scores, axis=0)
            m = jnp.max(s, axis=-1, keepdims=True)
            pb = jnp.exp(s - m).astype(BF16)
            for g, (c, u) in enumerate(units[g0:g0 + group]):
                b0 = slice(2 * g * qb, (2 * g + 1) * qb)
                b1 = slice((2 * g + 1) * qb, (2 * g + 2) * qb)
                pvl0 = _dot(pb[b0], jnp.where(vlane < C, values[g], one))
                pvl1 = _dot(pb[b1], jnp.where(vlane >= C, values[g], one))
                sel = rows(c + dil * u * qb, qb, dil)
                m_st[pi, sel, :] = jnp.where(first, m[b0], m[b1])
                l_st[pi, sel, :] = pltpu.roll(jnp.where(first, pvl1, pvl0), C, 1)
                acc_st[pi, sel, :] = jnp.where(first, pvl0, pvl1)
    m_all = m_st[0]
    for pi in range(1, len(DIL_PATTERNS)):
        m_all = jnp.maximum(m_all, m_st[pi])
    num = jnp.zeros((tq, LANES), F32)
    den = jnp.zeros((tq, LANES), F32)
    for pi in range(len(DIL_PATTERNS)):
        w = jnp.exp(m_st[pi] - m_all)
        num = num + w * acc_st[pi]
        den = den + w * l_st[pi]
    o_ref[0, 0] = num / den


def _dilated_attention(q, k, v):
    _, B, S, _ = q.shape
    t = DIL_TILE
    nq = S // t
    cur = pl.BlockSpec((1, 1, t, LANES), lambda b, p, i: (p, b, i, 0))
    prev = pl.BlockSpec((1, 1, t, LANES), lambda b, p, i: (p, b, jnp.maximum(i - 1, 0), 0))
    nxt = pl.BlockSpec((1, 1, t, LANES), lambda b, p, i: (p, b, jnp.minimum(i + 1, nq - 1), 0))
    return pl.pallas_call(
        functools.partial(_dil_kernel, seq=S), grid=(B, 2, nq),
        in_specs=[cur, prev, cur, nxt, prev, cur, nxt],
        out_specs=cur, out_shape=jax.ShapeDtypeStruct(q.shape, F32),
        scratch_shapes=[pltpu.VMEM((3 * t, LANES), F32), pltpu.VMEM((3 * t, LANES), F32),
                        pltpu.VMEM((3, t, LANES), F32), pltpu.VMEM((3, t, LANES), F32), pltpu.VMEM((3, t, LANES), F32)],
        compiler_params=_params(("parallel", "parallel", "parallel")), name="dilated_attn",
    )(q, k, k, k, v, v, v)


def _dot2(a, b):
    a_hi, a_lo = _split2(a)
    b_hi, b_lo = _split2(b)
    return _dot(a_hi, b_hi) + _dot(a_hi, b_lo) + _dot(a_lo, b_hi)


def _exact_rhs_dot(x, m):
    hi, mid, lo = _split3(x)
    return _dot(hi, m) + _dot(mid, m) + _dot(lo, m)


def _ssm_ops_kernel(lamr_row, lami_row, ldt_row, lamr_col, lami_col, ldt_col, brt_ref, bit_ref, crt_ref, cit_ref,
                    t_ref, p_ref, q_ref, al_ref):
    L, N, P = SSM_CHUNK, SSM_STATE, SSM_GROUP
    K = L * P
    lane_j = lax.broadcasted_iota(jnp.int32, (L, K), 1) // P
    row_j = lax.broadcasted_iota(jnp.int32, (L, K), 0)
    sel_r = lax.broadcasted_iota(jnp.int32, (K, L), 0) // P
    sel_j = lax.broadcasted_iota(jnp.int32, (K, L), 1)
    jcol = lax.broadcasted_iota(jnp.int32, (N, L), 1).astype(F32)
    jrow = lax.broadcasted_iota(jnp.int32, (L, N), 0).astype(F32)
    kcat, p_parts, q_parts, al_parts = [], [], [], []
    for d in range(2):
        expand = jnp.where(row_j == (lane_j if d == 0 else L - 1 - lane_j), 1.0, 0.0).astype(BF16)
        select = jnp.where(sel_j == (L - 1 - sel_r if d == 0 else sel_r), 1.0, 0.0).astype(BF16)
        lr_c, li_c, dt_c = lamr_col[d, 0], lami_col[d, 0], jnp.exp(ldt_col[d, 0])
        pmag = jnp.exp(jcol * (lr_c * dt_c))
        pr0 = _exact_rhs_dot(pmag * jnp.cos(jcol * (li_c * dt_c)), expand)
        pi0 = _exact_rhs_dot(pmag * jnp.sin(jcol * (li_c * dt_c)), expand)
        mag_c = jnp.exp(lr_c * dt_c)
        ar_c, ai_c = mag_c * jnp.cos(li_c * dt_c), mag_c * jnp.sin(li_c * dt_c)
        pr1, pi1 = ar_c * pr0 - ai_c * pi0, ar_c * pi0 + ai_c * pr0
        cr = jnp.concatenate([crt_ref[d, 0]] * (K // LANES), axis=1)
        ci = jnp.concatenate([cit_ref[d, 0]] * (K // LANES), axis=1)
        lr_r, li_r, dt_r = lamr_row[d, 0], lami_row[d, 0], jnp.exp(ldt_row[d, 0])
        mag_r = jnp.exp(lr_r * dt_r)
        ar_r, ai_r = mag_r * jnp.cos(li_r * dt_r), mag_r * jnp.sin(li_r * dt_r)
        den = lr_r * lr_r + li_r * li_r
        fr = ((ar_r - 1.0) * lr_r + ai_r * li_r) / den
        fi = (ai_r * lr_r - (ar_r - 1.0) * li_r) / den
        bbr = fr * brt_ref[d, 0] - fi * bit_ref[d, 0]
        bbi = fr * bit_ref[d, 0] + fi * brt_ref[d, 0]
        kcat.append(_dot2(bbr, pr0 * cr - pi0 * ci) - _dot2(bbi, pr0 * ci + pi0 * cr))
        q_parts.append((pr1 * cr - pi1 * ci, -(pi1 * cr + pr1 * ci)))
        rmag = jnp.exp(jrow * (lr_r * dt_r))
        prr = rmag * jnp.cos(jrow * (li_r * dt_r))
        pir = rmag * jnp.sin(jrow * (li_r * dt_r))
        prs, pis = _exact_lhs_dot(select, prr), _exact_lhs_dot(select, pir)
        bbr_t = jnp.concatenate([bbr] * L, axis=0)
        bbi_t = jnp.concatenate([bbi] * L, axis=0)
        p_parts.append((prs * bbr_t - pis * bbi_t, prs * bbi_t + pis * bbr_t))
        al_parts.append((ar_r * prr[L - 1:L] - ai_r * pir[L - 1:L], ar_r * pir[L - 1:L] + ai_r * prr[L - 1:L]))
    zeros = jnp.zeros((P, K), F32)
    ksum = (jnp.concatenate([zeros, kcat[0]], axis=1)
            + pltpu.roll(jnp.concatenate([kcat[1], zeros], axis=1), P, 1))
    per_tile = LANES // P
    for b in range(per_tile):
        rolled = ksum if b == 0 else pltpu.roll(ksum, P * b, 1)
        for a in range(L // per_tile):
            r = per_tile * a + b
            t_ref[0, P * r:P * (r + 1), :] = rolled[:, K - LANES * a:2 * K - LANES * a].astype(BF16)
    p_ref[0] = jnp.concatenate([p_parts[0][0], p_parts[1][0], p_parts[0][1], p_parts[1][1]], axis=1).astype(BF16)
    q_ref[0] = jnp.concatenate([q_parts[0][0], q_parts[1][0], q_parts[0][1], q_parts[1][1]], axis=0).astype(BF16)
    al_ref[0] = jnp.concatenate([jnp.concatenate([al_parts[0][0], al_parts[1][0]], axis=1),
                                 jnp.concatenate([al_parts[0][1], al_parts[1][1]], axis=1)], axis=0)


def _ssm_operators(lam_re, lam_im, log_dt, b_re, b_im, c_re, c_im):
    L, G, N, P = SSM_CHUNK, SSM_GROUPS, SSM_STATE, SSM_GROUP
    K = L * P
    rowf = lambda v: v.reshape(2, G, 1, N)
    colf = lambda v: v.reshape(2, G, N, 1)
    ldt = jnp.broadcast_to(log_dt[:, :, None], (2, G, N))
    tr = lambda v: jnp.swapaxes(v, 2, 3)
    ct = lambda v: jnp.tile(tr(v), (1, 1, 1, LANES // P))
    spec = lambda shape: pl.BlockSpec((2, 1) + shape, lambda g: (0, g, 0, 0))
    t_sum, p_cat, q_cat, al = pl.pallas_call(
        _ssm_ops_kernel, grid=(G,),
        in_specs=[spec((1, N))] * 3 + [spec((N, 1))] * 3 + [spec((P, N))] * 2 + [spec((N, LANES))] * 2,
        out_specs=[pl.BlockSpec((1, K, K), lambda g: (g, 0, 0)), pl.BlockSpec((1, K, 4 * N), lambda g: (g, 0, 0)),
                   pl.BlockSpec((1, 4 * N, K), lambda g: (g, 0, 0)), pl.BlockSpec((1, 2, 2 * N), lambda g: (g, 0, 0))],
        out_shape=[jax.ShapeDtypeStruct((G, K, K), BF16), jax.ShapeDtypeStruct((G, K, 4 * N), BF16),
                   jax.ShapeDtypeStruct((G, 4 * N, K), BF16), jax.ShapeDtypeStruct((G, 2, 2 * N), F32)],
        compiler_params=_params(("parallel",)), name="ssm_ops",
    )(rowf(lam_re), rowf(lam_im), rowf(ldt), colf(lam_re), colf(lam_im), colf(ldt),
      tr(b_re), tr(b_im), ct(c_re), ct(c_im))
    return t_sum, p_cat, q_cat, al[:, 0, :].reshape(1, G * 2 * N), al[:, 1, :].reshape(1, G * 2 * N)


def _ssm_state_in_kernel(u_ref, p_ref, xre_ref, xim_ref):
    x = _dot(u_ref[0], p_ref[0])
    xre_ref[...] = x[:, 0:LANES]
    xim_ref[...] = x[:, LANES:2 * LANES]


def _ssm_state_in(u_t, p_cat):
    G, M, K = u_t.shape
    tm = min(M, 256)
    return pl.pallas_call(
        _ssm_state_in_kernel, grid=(G, M // tm),
        in_specs=[pl.BlockSpec((1, tm, K), lambda g, m: (g, m, 0)),
                  pl.BlockSpec((1, K, 2 * LANES), lambda g, m: (g, 0, 0))],
        out_specs=[pl.BlockSpec((tm, LANES), lambda g, m: (m, g))] * 2,
        out_shape=[jax.ShapeDtypeStruct((M, G * LANES), F32)] * 2,
        compiler_params=_params(("parallel", "parallel")), name="ssm_state_in",
    )(u_t, p_cat)


def _ssm_scan_kernel(xre_ref, xim_ref, are_ref, aim_ref, hfre_ref, hfim_ref, hbre_ref, hbim_ref):
    nc = xre_ref.shape[1]
    width = xre_ref.shape[2]
    ar = are_ref[...]
    ai = aim_ref[...]
    lane = lax.broadcasted_iota(jnp.int32, (1, width), 1)
    is_fwd = (lane % LANES) < SSM_STATE

    def step(i, carry):
        hr, hi = carry
        ib = nc - 1 - i
        hfre_ref[0, pl.ds(i, 1), :] = hr
        hfim_ref[0, pl.ds(i, 1), :] = hi
        hbre_ref[0, pl.ds(ib, 1), :] = hr
        hbim_ref[0, pl.ds(ib, 1), :] = hi
        xr = jnp.where(is_fwd, xre_ref[0, pl.ds(i, 1), :], xre_ref[0, pl.ds(ib, 1), :])
        xi = jnp.where(is_fwd, xim_ref[0, pl.ds(i, 1), :], xim_ref[0, pl.ds(ib, 1), :])
        return ar * hr - ai * hi + xr, ar * hi + ai * hr + xi

    zero = jnp.zeros((1, width), F32)
    lax.fori_loop(0, nc, step, (zero, zero))


def _ssm_scan(xre, xim, al_re, al_im):
    B, nc, width = xre.shape
    blk = pl.BlockSpec((1, nc, width), lambda b: (b, 0, 0))
    vec = pl.BlockSpec((1, width), lambda b: (0, 0))
    return pl.pallas_call(
        _ssm_scan_kernel, grid=(B,),
        in_specs=[blk, blk, vec, vec], out_specs=[blk] * 4,
        out_shape=[jax.ShapeDtypeStruct((B, nc, width), F32)] * 4,
        compiler_params=_params(("parallel",)), name="ssm_scan",
    )(xre, xim, al_re, al_im)


def _ssm_out_kernel(u_ref, t_ref, hfre_ref, hfim_ref, hbre_ref, hbim_ref, q_ref, y_ref):
    tm = u_ref.shape[1]
    lane = lax.broadcasted_iota(jnp.int32, (tm, LANES), 1)
    is_fwd = lane < SSM_STATE
    h_re = jnp.where(is_fwd, hfre_ref[...], hbre_ref[...])
    h_im = jnp.where(is_fwd, hfim_ref[...], hbim_ref[...])
    h_cat = jnp.concatenate([h_re, h_im], axis=1).astype(BF16)
    y_ref[0] = _dot(u_ref[0], t_ref[0]) + _dot(h_cat, q_ref[0])


def _ssm_out(u_t, t_sum, hs, q_cat):
    G, M, K = u_t.shape
    tm = min(M, 256)
    hblk = pl.BlockSpec((tm, LANES), lambda g, m: (m, g))
    return pl.pallas_call(
        _ssm_out_kernel, grid=(G, M // tm),
        in_specs=[pl.BlockSpec((1, tm, K), lambda g, m: (g, m, 0)),
                  pl.BlockSpec((1, K, K), lambda g, m: (g, 0, 0)),
                  hblk, hblk, hblk, hblk,
                  pl.BlockSpec((1, 2 * LANES, K), lambda g, m: (g, 0, 0))],
        out_specs=pl.BlockSpec((1, tm, K), lambda g, m: (g, m, 0)),
        out_shape=jax.ShapeDtypeStruct((G, M, K), F32),
        compiler_params=_params(("parallel", "parallel")), name="ssm_out",
    )(u_t, t_sum, *hs, q_cat)


def _s5_scan(u, ops):
    t_sum, p_cat, q_cat, al_re, al_im = ops
    B, S, _ = u.shape
    L, G, P = SSM_CHUNK, SSM_GROUPS, SSM_GROUP
    nc = S // L
    u_t = u.reshape(B, nc, L, G, P).transpose(3, 0, 1, 2, 4).reshape(G, B * nc, L * P).astype(BF16)
    xre, xim = _ssm_state_in(u_t, p_cat)
    width = G * LANES
    hs = _ssm_scan(xre.reshape(B, nc, width), xim.reshape(B, nc, width), al_re, al_im)
    hfre, hfim, hbre, hbim = (h.reshape(B * nc, width) for h in hs)
    y = _ssm_out(u_t, t_sum, (hfre, hfim, hbre, hbim), q_cat)
    return y.reshape(G, B, nc, L, P).transpose(1, 2, 3, 0, 4).reshape(B, S, G * P)


def _na_bias_kernel(rpb_ref, out_ref):
    h = pl.program_id(0)
    n_dr, n_dc = 2 * NA_WIN_R - 1, 2 * NA_WIN_C - 1
    qc = lax.broadcasted_iota(jnp.int32, (GRID_W, GRID_W), 0)
    kc = lax.broadcasted_iota(jnp.int32, (GRID_W, GRID_W), 1)
    dc = jnp.clip(kc - qc + (NA_WIN_C - 1), 0, n_dc - 1)
    cstart = jnp.clip(qc - NA_WIN_C // 2, 0, GRID_W - NA_WIN_C)
    col_ok = (kc >= cstart) & (kc < cstart + NA_WIN_C)
    base = []
    for dr in range(n_dr):
        acc = jnp.zeros((GRID_W, GRID_W), F32)
        for d in range(n_dc):
            acc = jnp.where(dc == d, rpb_ref[(h * n_dr + dr) * n_dc + d], acc)
        base.append(jnp.where(col_ok, acc, NEG))
    for off in range(NA_WIN_R):
        out_ref[0, off] = jnp.concatenate([base[w - off + NA_WIN_R - 1] for w in range(NA_WIN_R)], axis=1)


def _na_bias_table(rpb):
    return pl.pallas_call(
        _na_bias_kernel, grid=(HEADS,),
        in_specs=[pl.BlockSpec(memory_space=pltpu.SMEM)],
        out_specs=pl.BlockSpec((1, NA_WIN_R, GRID_W, NA_WIN_R * GRID_W), lambda h: (h, 0, 0, 0)),
        out_shape=jax.ShapeDtypeStruct((HEADS, NA_WIN_R, GRID_W, NA_WIN_R * GRID_W), F32),
        compiler_params=_params(("parallel",)), name="na_bias",
    )(rpb.astype(F32).reshape(-1))


def _na_kernel(q_ref, kp_ref, kc_ref, kn_ref, vp_ref, vc_ref, vn_ref, bias_ref, o_ref, kbuf, vbuf, *, rows):
    rb = pl.program_id(1)
    Wg = GRID_W
    C = HEAD_DIM
    halo = NA_HALO_ROWS * Wg
    cur = NA_ROWS_PER_STEP * Wg
    win = NA_WIN_R * Wg
    kbuf[0:halo, :] = kp_ref[0]
    kbuf[halo:halo + cur, :] = kc_ref[0]
    kbuf[halo + cur:2 * halo + cur, :] = kn_ref[0]
    vbuf[0:halo, :] = vp_ref[0]
    vbuf[halo:halo + cur, :] = vc_ref[0]
    vbuf[halo + cur:2 * halo + cur, :] = vn_ref[0]
    r0 = rb * NA_ROWS_PER_STEP
    pairs = HEADS // 2
    assert LANES == 2 * C
    klane = lax.broadcasted_iota(jnp.int32, (win, LANES), 1)
    olane = lax.broadcasted_iota(jnp.int32, (Wg, LANES), 1)
    zero = jnp.zeros((win, LANES), BF16)
    one = jnp.ones((win, LANES), BF16)
    group = 2
    for j0 in range(0, NA_ROWS_PER_STEP, group):
        scores, values = [], []
        for j in range(j0, j0 + group):
            r = r0 + j
            rs = jnp.clip(r - NA_WIN_R // 2, 0, rows - NA_WIN_R)
            off = r - rs
            start = pl.multiple_of((rs - r0 + NA_HALO_ROWS) * Wg, Wg)
            kw = kbuf[pl.ds(start, win), :]
            values.append(vbuf[pl.ds(start, win), :])
            for pr in range(pairs):
                tile = slice(pr * LANES, (pr + 1) * LANES)
                k_bd = jnp.concatenate([jnp.where(klane < C, kw[:, tile], zero),
                                        jnp.where(klane >= C, kw[:, tile], zero)], axis=0)
                s2 = _dot_nt(q_ref[0, j * Wg:(j + 1) * Wg, tile], k_bd)
                scores.append(s2[:, 0:win] + bias_ref[2 * pr, off])
                scores.append(s2[:, win:2 * win] + bias_ref[2 * pr + 1, off])
        s = jnp.concatenate(scores, axis=0)
        pb = jnp.exp(s - jnp.max(s, axis=-1, keepdims=True)).astype(BF16)
        for g in range(group):
            outs = []
            for pr in range(pairs):
                tile = slice(pr * LANES, (pr + 1) * LANES)
                b0 = slice((g * HEADS + 2 * pr) * Wg, (g * HEADS + 2 * pr + 1) * Wg)
                b1 = slice((g * HEADS + 2 * pr + 1) * Wg, (g * HEADS + 2 * pr + 2) * Wg)
                pvl0 = _dot(pb[b0], jnp.where(klane < C, values[g][:, tile], one))
                pvl1 = _dot(pb[b1], jnp.where(klane >= C, values[g][:, tile], one))
                first = olane < C
                outs.append(jnp.where(first, pvl0, pvl1) / pltpu.roll(jnp.where(first, pvl1, pvl0), C, 1))
            o_ref[0, (j0 + g) * Wg:(j0 + g + 1) * Wg, :] = jnp.concatenate(outs, axis=1)


def _neighbourhood_attention(q, k, v, bias):
    B, S, W = q.shape
    rows = S // GRID_W
    cur = NA_ROWS_PER_STEP * GRID_W
    halo = NA_HALO_ROWS * GRID_W
    nrb = S // cur
    per = cur // halo
    nh = S // halo
    blk = pl.BlockSpec((1, cur, W), lambda b, i: (b, i, 0))
    prev = pl.BlockSpec((1, halo, W), lambda b, i: (b, jnp.maximum(i * per - 1, 0), 0))
    nxt = pl.BlockSpec((1, halo, W), lambda b, i: (b, jnp.minimum((i + 1) * per, nh - 1), 0))
    bspec = pl.BlockSpec(bias.shape, lambda b, i: (0, 0, 0, 0))
    return pl.pallas_call(
        functools.partial(_na_kernel, rows=rows), grid=(B, nrb),
        in_specs=[blk, prev, blk, nxt, prev, blk, nxt, bspec],
        out_specs=blk, out_shape=jax.ShapeDtypeStruct((B, S, W), F32),
        scratch_shapes=[pltpu.VMEM((cur + 2 * halo, W), BF16), pltpu.VMEM((cur + 2 * halo, W), BF16)],
        compiler_params=_params(("parallel", "parallel")), name="neighbourhood_attn",
    )(q, k, k, k, v, v, v, bias)


def _post_kernel(x_ref, of_ref, ob_ref, gate_ref, yb_ref, ys_ref, u_ref, yd_ref,
                 dnw_ref, bd_ref, dskip_ref, gluw_ref, glub_ref, wout_ref, ln2_ref, w1_ref, w2_ref, fnw_ref,
                 out_ref, *, final):
    x = x_ref[...]
    o = of_ref[...] + ob_ref[...]
    ms = _head_sums(o * o, bd_ref[...]) * (1.0 / HEAD_DIM)
    g = gate_ref[...]
    y_a = o * lax.rsqrt(ms + EPS) * dnw_ref[...] * (g * _sigmoid(g))
    y = ys_ref[...] + dskip_ref[...] * u_ref[...]
    z = 0.5 * y * (1.0 + jnp.tanh(math.sqrt(2.0 / math.pi) * (y + 0.044715 * (y * y * y))))
    y_c = z * _sigmoid(_dot(z.astype(BF16), gluw_ref[...]) + glub_ref[...])
    mix = jnp.concatenate([y_a, yb_ref[0], yb_ref[1], y_c, yd_ref[...]], axis=1).astype(BF16)
    x1 = x + _dot(mix, wout_ref[...])
    hb = _rms(x1, ln2_ref[...]).astype(BF16)
    ffc = D_MODEL
    acc = x1
    for c in range(D_FF // ffc):
        a = jnp.maximum(_dot(hb, w1_ref[:, c * ffc:(c + 1) * ffc]), 0.0)
        acc = acc + _dot((a * a).astype(BF16), w2_ref[c * ffc:(c + 1) * ffc, :])
    if final:
        acc = _rms(acc, fnw_ref[...])
    out_ref[...] = acc


def _post(x, o_f, o_b, gate, y_b, y_s, u, y_d, wts, final):
    T = x.shape[0]
    tm = min(TOKEN_TILE, T)
    W = WIDTH
    row = lambda n: pl.BlockSpec((tm, n), lambda i: (i, 0))
    const = lambda a: pl.BlockSpec(a.shape, lambda i: (0, 0), pipeline_mode=pl.Buffered(1))
    return pl.pallas_call(
        functools.partial(_post_kernel, final=final), grid=(T // tm,),
        in_specs=([row(D_MODEL)] + [row(W)] * 3 + [pl.BlockSpec((2, tm, LANES), lambda i: (0, i, 0))] + [row(W)] * 3
                  + [const(a) for a in wts]),
        out_specs=row(D_MODEL), out_shape=jax.ShapeDtypeStruct((T, D_MODEL), F32),
        compiler_params=_params(("parallel",)), name="post",
    )(x, o_f, o_b, gate, y_b, y_s, u, y_d, *wts)


def _rope_tables(seq):
    inv_freq = ROPE_THETA ** (-jnp.arange(0, HEAD_DIM, 2, dtype=F32) / HEAD_DIM)
    ang = jnp.arange(seq, dtype=F32)[:, None] * inv_freq[None, :]
    cos, sin = jnp.cos(ang), jnp.sin(ang)
    cos_t = jnp.tile(jnp.concatenate([cos, cos], axis=1), (1, HEADS))
    sin_t = jnp.tile(jnp.concatenate([-sin, sin], axis=1), (1, HEADS))
    return cos_t, sin_t


def _layer_weights(i, p):
    W = WIDTH
    w = p['w_in'][i]
    o_gate, o_a, o_b, o_dil, o_ssm, o_na = 3 * W, 4 * W, 4 * W + 2 * HEADS, 4 * W + 4 * HEADS, 7 * W + 4 * HEADS, 8 * W + 4 * HEADS
    pad = jnp.zeros((D_MODEL, LANES - 4 * HEADS), F32)
    w_in = jnp.concatenate([w[:, 0:o_a], w[:, o_dil:o_na + 3 * W], w[:, o_a:o_dil], pad], axis=1).astype(BF16)
    assert w_in.shape[1] == IN_COLS_PADDED
    lane_pad = lambda v: jnp.concatenate([v.reshape(1, -1).astype(F32), jnp.zeros((1, LANES - v.size), F32)], axis=1)
    idx = np.arange(W)
    bd = jnp.asarray((idx[:, None] // HEAD_DIM) == (idx[None, :] // HEAD_DIM), BF16)
    row = lambda v: v.reshape(1, -1).astype(F32)
    return dict(
        ln1=row(p['ln1_w'][i]), w_in=w_in,
        conv_w=p['dn_conv_w'][i].astype(F32), alog=lane_pad(p['dn_a_log'][i]), dtb=lane_pad(p['dn_dt_bias'][i]),
        bd=bd,
        ssm=_ssm_operators(*(p[n][i].astype(F32) for n in
                             ('ssm_lam_re', 'ssm_lam_im', 'ssm_log_dt', 'ssm_b_re', 'ssm_b_im', 'ssm_c_re', 'ssm_c_im'))),
        na_bias=_na_bias_table(p['na_rpb'][i]),
        post=(row(jnp.tile(p['dn_norm_w'][i], HEADS)), bd, row(p['ssm_d'][i]), p['ssm_glu_w'][i].astype(BF16),
              row(p['ssm_glu_b'][i]), p['w_out'][i].astype(BF16), row(p['ln2_w'][i]),
              p['w_ff1'][i].astype(BF16), p['w_ff2'][i].astype(BF16), row(p['final_norm_w'])),
    )


def _trunk(x, layers):
    B, S, D = x.shape
    T = B * S
    cos_t, sin_t = _rope_tables(S)
    xt = x.reshape(T, D)
    seq3 = lambda a: a.reshape(B, S, a.shape[-1])
    flat = lambda a: a.reshape(T, a.shape[-1])
    for i, lw in enumerate(layers):
        q, k, v, gb, z_gate, dq, dk, dv, z_ssm, nq, nk, nv = _in_proj(
            xt, lw['ln1'], lw['w_in'], cos_t, sin_t, lw['conv_w'], lw['alog'], lw['dtb'], lw['bd'], S)
        o_f, o_b = _dn_scan(seq3(q), seq3(k), seq3(v), seq3(gb))
        pairs = lambda a: a.reshape(2, B, S, LANES)
        y_b = _dilated_attention(pairs(dq), pairs(dk), pairs(dv)).reshape(2, T, LANES)
        y_s = _s5_scan(seq3(z_ssm), lw['ssm'])
        y_d = _neighbourhood_attention(seq3(nq), seq3(nk), seq3(nv), lw['na_bias'])
        xt = _post(xt, flat(o_f), flat(o_b), z_gate, y_b, flat(y_s), z_ssm, flat(y_d),
                   lw['post'], final=(i == len(layers) - 1))
    return xt.reshape(B, S, D)


def kernel(x_prompt, x_sample, ln1_w, w_in, dn_conv_w, dn_a_log, dn_dt_bias, dn_norm_w, ssm_lam_re, ssm_lam_im,
           ssm_log_dt, ssm_b_re, ssm_b_im, ssm_c_re, ssm_c_im, ssm_d, ssm_glu_w, ssm_glu_b, na_rpb, w_out,
           ln2_w, w_ff1, w_ff2, final_norm_w):
    p = dict(ln1_w=ln1_w, w_in=w_in, dn_conv_w=dn_conv_w, dn_a_log=dn_a_log, dn_dt_bias=dn_dt_bias,
             dn_norm_w=dn_norm_w, ssm_lam_re=ssm_lam_re, ssm_lam_im=ssm_lam_im, ssm_log_dt=ssm_log_dt,
             ssm_b_re=ssm_b_re, ssm_b_im=ssm_b_im, ssm_c_re=ssm_c_re, ssm_c_im=ssm_c_im, ssm_d=ssm_d,
             ssm_glu_w=ssm_glu_w, ssm_glu_b=ssm_glu_b, na_rpb=na_rpb, w_out=w_out, ln2_w=ln2_w,
             w_ff1=w_ff1, w_ff2=w_ff2, final_norm_w=final_norm_w)
    layers = [_layer_weights(i, p) for i in range(DEPTH)]
    return (_trunk(x_prompt, layers), _trunk(x_sample, layers))
```

```python
import functools
import math

import numpy as np
import jax
import jax.numpy as jnp
from jax import lax
from jax.experimental import pallas as pl
from jax.experimental.pallas import tpu as pltpu

F32 = jnp.float32
BF16 = jnp.bfloat16

D_MODEL = 1024
DEPTH = 2
HEAD_DIM = 64
HEADS = 4
WIDTH = HEADS * HEAD_DIM
DN_CHUNK = 64
CONV_WIDTH = 5
DIL_PATTERNS = ((128, 1), (512, 4), (2048, 16))
SSM_GROUP = 16
SSM_GROUPS = 16
SSM_STATE = 64
SSM_CHUNK = 64
GRID_W = 64
NA_WIN_R = 8
NA_WIN_C = 16
D_FF = 4 * D_MODEL
ROPE_THETA = 10000.0
EPS = 1e-6
NEG = -1e30

LANES = 128
TOKEN_TILE = 512
DIL_TILE = 1024
DIL_BAND = 64
NA_ROWS_PER_STEP = 8
NA_HALO_ROWS = 4
IN_COLS_PADDED = 23 * LANES
VMEM_LIMIT = 56 * 2**20


def _params(sem):
    return pltpu.CompilerParams(dimension_semantics=sem, vmem_limit_bytes=VMEM_LIMIT)


def _dot(a, b):
    return jnp.dot(a, b, preferred_element_type=F32)


def _dot_nt(a, b):
    return lax.dot_general(a, b, (((1,), (1,)), ((), ())), preferred_element_type=F32)


def _bmm(a, b):
    return lax.dot_general(a, b, (((2,), (1,)), ((0,), (0,))), preferred_element_type=F32)


def _bmm_nt(a, b):
    return lax.dot_general(a, b, (((2,), (2,)), ((0,), (0,))), preferred_element_type=F32)


def _split2(x):
    hi = x.astype(BF16)
    lo = (x - hi.astype(F32)).astype(BF16)
    return hi, lo


def _split3(x):
    hi = x.astype(BF16)
    r = x - hi.astype(F32)
    mid = r.astype(BF16)
    lo = (r - mid.astype(F32)).astype(BF16)
    return hi, mid, lo


def _exact_lhs_dot(m, x):
    hi, mid, lo = _split3(x)
    return _dot(m, hi) + _dot(m, mid) + _dot(m, lo)


def _head_sums(sq, bd):
    hi, lo = _split2(sq)
    return _dot(hi, bd) + _dot(lo, bd)


def _sigmoid(x):
    return 1.0 / (1.0 + jnp.exp(-x))


def _rms(x, w):
    ms = jnp.mean(x * x, axis=-1, keepdims=True)
    return x * lax.rsqrt(ms + EPS) * w


def _group_masks(rows):
    lane_group = lax.broadcasted_iota(jnp.int32, (rows, LANES), 1) // SSM_GROUP
    return [lane_group == b for b in range(LANES // SSM_GROUP)]


def _chunk_flatten(zs, sbuf, ut_ref):
    L, P = SSM_CHUNK, SSM_GROUP
    per = LANES // P
    nck = zs.shape[0] // L
    masks = _group_masks(nck)
    for h in range(zs.shape[1] // LANES):
        sbuf[h] = zs[:, h * LANES:(h + 1) * LANES]
    for h in range(zs.shape[1] // LANES):
        out = [[jnp.zeros((nck, LANES), F32) for _ in range(L // per)] for _ in range(per)]
        for a in range(L // per):
            for b in range(per):
                v = sbuf[h, pl.ds(per * a + b, nck, stride=L), :]
                for d in range(per):
                    g = (b - d) % per
                    out[g][a] = jnp.where(masks[b], v if d == 0 else pltpu.roll(v, P * d, 1), out[g][a])
        for g in range(per):
            ut_ref[h * per + g] = jnp.concatenate(out[g], axis=1)


def _chunk_unflatten(yt_ref, ybuf):
    L, P = SSM_CHUNK, SSM_GROUP
    per = LANES // P
    nck = yt_ref.shape[1]
    assert nck == per
    masks = _group_masks(nck)
    halves = yt_ref.shape[0] // per
    out = []
    for h in range(halves):
        for a in range(L // per):
            rows = [jnp.zeros((nck, LANES), F32) for _ in range(per)]
            for g in range(per):
                blk = yt_ref[h * per + g, :, a * LANES:(a + 1) * LANES]
                for d in range(per):
                    b = (g - d) % per
                    rows[b] = jnp.where(masks[g], blk if d == 0 else pltpu.roll(blk, P * d, 1), rows[b])
            for b in range(per):
                ybuf[h, L * a + per * b:L * a + per * (b + 1), :] = rows[b]
        out.append(jnp.concatenate([ybuf[h, pl.ds(L * a + c, per, stride=per), :]
                                    for c in range(nck) for a in range(L // per)], axis=0))
    return jnp.concatenate(out, axis=1)


def _in_proj_kernel(x_ref, xp_ref, xn_ref, lnw_ref, w_ref, cos_ref, sin_ref, cw_ref, alog_ref, dtb_ref, bd_ref,
                    q_ref, k_ref, v_ref, gb_ref, gate_ref, dq_ref, dk_ref, dv_ref, ssm_ref, ut_ref, nq_ref, nk_ref, nv_ref,
                    buf, sbuf, *, per_seq):
    lnw = lnw_ref[...]
    z = _dot(_rms(x_ref[...], lnw).astype(BF16), w_ref[...])
    tm = z.shape[0]
    W = WIDTH
    gate_ref[...] = z[:, 3 * W:4 * W]
    cos = cos_ref[...]
    sin = sin_ref[...]
    lane = lax.broadcasted_iota(jnp.int32, (tm, W), 1)
    first_half = (lane % HEAD_DIM) < (HEAD_DIM // 2)

    def rope(t):
        rot = jnp.where(first_half, pltpu.roll(t, W - HEAD_DIM // 2, 1), pltpu.roll(t, HEAD_DIM // 2, 1))
        return t * cos + rot * sin

    scale = HEAD_DIM ** -0.5
    for ref, t in ((dq_ref, rope(z[:, 4 * W:5 * W]) * scale), (dk_ref, rope(z[:, 5 * W:6 * W])),
                   (dv_ref, z[:, 6 * W:7 * W])):
        ref[0] = t[:, 0:LANES]
        ref[1] = t[:, LANES:2 * LANES]
    zs = z[:, 7 * W:8 * W]
    ssm_ref[...] = zs
    _chunk_flatten(zs, sbuf, ut_ref)
    nq_ref[...] = (z[:, 8 * W:9 * W] * scale).astype(BF16)
    nk_ref[...] = z[:, 9 * W:10 * W].astype(BF16)
    nv_ref[...] = z[:, 10 * W:11 * W].astype(BF16)

    j = pl.program_id(0) % per_seq
    half = CONV_WIDTH // 2
    w_dn = w_ref[:, 0:3 * W]
    zp = _dot(_rms(xp_ref[...], lnw).astype(BF16), w_dn)
    zn = _dot(_rms(xn_ref[...], lnw).astype(BF16), w_dn)
    buf[0:8, :] = jnp.where(j > 0, zp, 0.0)
    buf[8:8 + tm, :] = z[:, 0:3 * W]
    buf[8 + tm:16 + tm, :] = jnp.where(j < per_seq - 1, zn, 0.0)
    cw = cw_ref[...]
    acc = buf[8 - half:8 - half + tm, :] * cw[0:1, :]
    for t in range(1, CONV_WIDTH):
        acc = acc + buf[8 - half + t:8 - half + t + tm, :] * cw[t:t + 1, :]
    y = acc * _sigmoid(acc)
    bd = bd_ref[...]
    q = y[:, 0:W]
    k = y[:, W:2 * W]
    q_ref[...] = q * lax.rsqrt(_head_sums(q * q, bd) + EPS) * scale
    k_ref[...] = k * lax.rsqrt(_head_sums(k * k, bd) + EPS)
    v_ref[...] = y[:, 2 * W:3 * W]
    ab = z[:, 11 * W:11 * W + LANES]
    t = ab + dtb_ref[...]
    softplus = jnp.maximum(t, 0.0) + jnp.log(1.0 + jnp.exp(-jnp.abs(t)))
    g = -jnp.exp(alog_ref[...]) * softplus
    lane_ab = lax.broadcasted_iota(jnp.int32, ab.shape, 1)
    gb_ref[...] = jnp.where(lane_ab < 2 * HEADS, g, _sigmoid(ab))


def _in_proj(x, lnw, w, cos_t, sin_t, conv_w, alog, dtb, bd, seq):
    T = x.shape[0]
    tm = min(TOKEN_TILE, seq)
    per_seq = seq // tm
    r8 = tm // 8
    W = WIDTH
    row = lambda n: pl.BlockSpec((tm, n), lambda i: (i, 0))
    const = lambda a: pl.BlockSpec(a.shape, lambda i: (0, 0))
    tab = pl.BlockSpec((tm, W), lambda i: (i % per_seq, 0))
    prev = pl.BlockSpec((8, D_MODEL), lambda i: (jnp.maximum(i * r8 - 1, 0), 0))
    nxt = pl.BlockSpec((8, D_MODEL), lambda i: (jnp.minimum((i + 1) * r8, T // 8 - 1), 0))
    pair = pl.BlockSpec((2, tm, LANES), lambda i: (0, i, 0))
    pair_shape = jax.ShapeDtypeStruct((2, T, LANES), F32)
    f32 = lambda n: jax.ShapeDtypeStruct((T, n), F32)
    bf16 = lambda n: jax.ShapeDtypeStruct((T, n), BF16)
    nck = tm // SSM_CHUNK
    ut_shape = jax.ShapeDtypeStruct((SSM_GROUPS, T // SSM_CHUNK, SSM_CHUNK * SSM_GROUP), F32)
    ut_spec = pl.BlockSpec((SSM_GROUPS, nck, SSM_CHUNK * SSM_GROUP), lambda i: (0, i, 0))
    out_shapes = [f32(W), f32(W), f32(W), f32(LANES), f32(W), pair_shape, pair_shape, pair_shape, f32(W), ut_shape,
                  bf16(W), bf16(W), bf16(W)]
    out_specs = [row(W), row(W), row(W), row(LANES), row(W), pair, pair, pair, row(W), ut_spec,
                 row(W), row(W), row(W)]
    return pl.pallas_call(
        functools.partial(_in_proj_kernel, per_seq=per_seq), grid=(T // tm,),
        in_specs=[row(D_MODEL), prev, nxt, const(lnw), const(w), tab, tab,
                  const(conv_w), const(alog), const(dtb), const(bd)],
        out_specs=out_specs, out_shape=out_shapes,
        scratch_shapes=[pltpu.VMEM((tm + 16, 3 * W), F32), pltpu.VMEM((W // LANES, tm, LANES), F32)],
        compiler_params=_params(("parallel",)), name="in_proj",
    )(x, x, x, lnw, w, cos_t, sin_t, conv_w, alog, dtb, bd)


def _dn_scan_kernel(qf_ref, kf_ref, vf_ref, gf_ref, qr_ref, kr_ref, vr_ref, gr_ref, of_ref, or_ref,
                    s_scr, u_scr, w_scr, qg_scr, at_scr, kdt_scr, *, nc):
    C = DN_CHUNK
    tb = nc * C
    H = HEADS
    assert LANES == 2 * C

    @pl.when(pl.program_id(1) == 0)
    def _():
        s_scr[...] = jnp.zeros_like(s_scr)

    r = lax.broadcasted_iota(jnp.int32, (tb, tb), 0)
    c = lax.broadcasted_iota(jnp.int32, (tb, tb), 1)
    same = (r // C) == (c // C)
    ri = lax.broadcasted_iota(jnp.int32, (LANES, LANES), 0)
    ci = lax.broadcasted_iota(jnp.int32, (LANES, LANES), 1)
    eye128 = jnp.where(ri == ci, 1.0, 0.0).astype(BF16)
    same_head = (ri < C) == (ci < C)
    lane = lax.broadcasted_iota(jnp.int32, (tb, LANES), 1)
    first = lane < C
    ii = lax.broadcasted_iota(jnp.int32, (tb, LANES), 0) % C
    jj = lane % C
    lane_c = lax.broadcasted_iota(jnp.int32, (1, LANES), 1)

    def block_diag(x):
        ln = lax.broadcasted_iota(jnp.int32, x.shape, x.ndim - 1)
        zero = jnp.zeros_like(x)
        return jnp.concatenate([jnp.where(ln < C, x, zero), jnp.where(ln >= C, x, zero)], axis=x.ndim - 2)

    pick = lambda a, col: jnp.where(first, a[:, col:col + 1], a[:, col + 1:col + 2])
    parts = {name: [] for name in ('q', 'k', 'v', 'gam', 'gtot', 'beta', 'grow', 'incl', 'strict')}
    gtots = []
    for d, (q_ref, k_ref, v_ref, gb_ref) in enumerate(((qf_ref, kf_ref, vf_ref, gf_ref),
                                                       (qr_ref, kr_ref, vr_ref, gr_ref))):
        rev = d == 1
        gb = gb_ref[0]
        cum = same & ((c >= r) if rev else (c <= r))
        gam = _exact_lhs_dot(jnp.where(cum, 1.0, 0.0).astype(BF16), gb)
        gam3 = gam.reshape(nc, C, LANES)
        last = gam3[:, 0:1, :] if rev else gam3[:, C - 1:C, :]
        gtot = jnp.broadcast_to(last, (nc, C, LANES)).reshape(tb, LANES)
        gtots.append(gtot)
        g_hi, g_mid, g_lo = _split3(gam)
        gam_t = _dot_nt(eye128, g_hi) + _dot_nt(eye128, g_mid) + _dot_nt(eye128, g_lo)
        for pr in range(H // 2):
            ch0 = d * H + 2 * pr
            tile = slice(pr * LANES, (pr + 1) * LANES)
            parts['q'].append(q_ref[0, :, tile])
            parts['k'].append(k_ref[0, :, tile])
            parts['v'].append(v_ref[0, :, tile])
            parts['gam'].append(pick(gam, ch0))
            parts['gtot'].append(pick(gtot, ch0))
            parts['beta'].append(pick(gb, 2 * H + ch0))
            parts['grow'].append(jnp.concatenate(
                [jnp.concatenate([jnp.broadcast_to(gam_t[ch0:ch0 + 1, cc * C:(cc + 1) * C], (C, C)),
                                  jnp.broadcast_to(gam_t[ch0 + 1:ch0 + 2, cc * C:(cc + 1) * C], (C, C))], axis=1)
                 for cc in range(nc)], axis=0))
            parts['incl'].append((jj >= ii) if rev else (jj <= ii))
            parts['strict'].append((jj > ii) if rev else (jj < ii))
    stk = {name: jnp.concatenate(vals, axis=0) for name, vals in parts.items()}
    slots = 2 * (H // 2)
    rows_all = slots * tb
    chunks = lambda x: x.astype(BF16).reshape(slots * nc, C, x.shape[-1])
    ii_all = jnp.concatenate([ii] * slots, axis=0)
    jj_all = jnp.concatenate([jj] * slots, axis=0)
    qp, kp, vp, gam_c, beta = stk['q'], stk['k'], stk['v'], stk['gam'], stk['beta']
    dec = jnp.exp(jnp.where(stk['incl'], gam_c - stk['grow'], NEG))
    kb = kp * beta
    k_bd = block_diag(chunks(kp))
    a_raw = _bmm_nt(chunks(kb), k_bd).reshape(rows_all, LANES)
    qk = _bmm_nt(chunks(qp), k_bd).reshape(rows_all, LANES)
    lm = jnp.where(stk['strict'], a_raw * dec, 0.0)
    attn = qk * dec
    t_inv = jnp.where(ii_all == jj_all, 1.0, 0.0) - jnp.where((ii_all // 2) == (jj_all // 2), lm, 0.0)
    n = 2
    while n < C:
        off = ((ii_all // (2 * n)) == (jj_all // (2 * n))) & ((ii_all // n) != (jj_all // n))
        t3 = chunks(t_inv)
        bt = _bmm(chunks(jnp.where(off, lm, 0.0)), block_diag(t3))
        t_inv = t_inv - _bmm(t3, block_diag(bt.astype(BF16))).reshape(rows_all, LANES)
        n *= 2
    eg = jnp.exp(gam_c)
    rhs = jnp.concatenate([block_diag(chunks(vp * beta)), block_diag(chunks(kb * eg))], axis=2)
    sol = _bmm(chunks(t_inv), rhs).reshape(rows_all, 2 * LANES)
    u_scr[...] = sol[:, 0:LANES].reshape(slots, tb, LANES)
    w_scr[...] = sol[:, LANES:2 * LANES].astype(BF16).reshape(slots, tb, LANES)
    qg_scr[...] = (qp * eg).astype(BF16).reshape(slots, tb, LANES)
    at_scr[...] = attn.astype(BF16).reshape(slots, tb, LANES)
    kd = (kp * jnp.exp(stk['gtot'] - gam_c)).astype(BF16)
    for slot in range(slots):
        kdt_scr[slot] = _dot_nt(eye128, kd[slot * tb:(slot + 1) * tb]).astype(BF16)

    for step in range(nc):
        rng = []
        for slot in range(slots):
            cc = nc - 1 - step if slot >= H // 2 else step
            rng.append((cc * C, (cc + 1) * C))
        take = lambda ref: jnp.stack([ref[slot, lo:hi, :] for slot, (lo, hi) in enumerate(rng)], axis=0)
        s = s_scr[...]
        sb = s.astype(BF16)
        v_new = take(u_scr) - _bmm(take(w_scr), sb)
        vb = v_new.astype(BF16)
        o = _bmm(take(qg_scr), sb) + _bmm(take(at_scr), block_diag(vb))
        gl = []
        for slot, (lo, hi) in enumerate(rng):
            d, pr = divmod(slot, H // 2)
            ch0 = d * H + 2 * pr
            o_ref = or_ref if d == 1 else of_ref
            o_ref[0, lo:hi, pr * LANES:(pr + 1) * LANES] = o[slot]
            gt = gtots[d]
            gl.append(jnp.exp(jnp.where(lane_c < C, gt[lo:lo + 1, ch0:ch0 + 1], gt[lo:lo + 1, ch0 + 1:ch0 + 2])))
        kdt = jnp.stack([kdt_scr[slot, :, lo:hi] for slot, (lo, hi) in enumerate(rng)], axis=0)
        upd = _bmm(kdt, vb)
        s_scr[...] = s * jnp.stack(gl, axis=0) + jnp.where(same_head[None], upd, 0.0)


def _dn_scan(q, k, v, gb):
    B, S, W = q.shape
    tb = min(TOKEN_TILE, S)
    nb = S // tb
    nc = tb // DN_CHUNK
    fwd = lambda n: pl.BlockSpec((1, tb, n), lambda b, i: (b, i, 0))
    bwd = lambda n: pl.BlockSpec((1, tb, n), lambda b, i: (b, nb - 1 - i, 0))
    slots = HEADS
    return pl.pallas_call(
        functools.partial(_dn_scan_kernel, nc=nc), grid=(B, nb),
        in_specs=[fwd(W), fwd(W), fwd(W), fwd(LANES), bwd(W), bwd(W), bwd(W), bwd(LANES)],
        out_specs=[fwd(W), bwd(W)], out_shape=[jax.ShapeDtypeStruct((B, S, W), F32)] * 2,
        scratch_shapes=[pltpu.VMEM((slots, LANES, LANES), F32), pltpu.VMEM((slots, tb, LANES), F32),
                        pltpu.VMEM((slots, tb, LANES), BF16), pltpu.VMEM((slots, tb, LANES), BF16),
                        pltpu.VMEM((slots, tb, LANES), BF16), pltpu.VMEM((slots, LANES, tb), BF16)],
        compiler_params=_params(("parallel", "arbitrary")), name="dn_scan",
    )(q, k, v, gb, q, k, v, gb)


def _dil_kernel(q_ref, kp_ref, kc_ref, kn_ref, vp_ref, vc_ref, vn_ref, o_ref,
                kwin, vwin, m_st, l_st, acc_st, *, seq):
    i = pl.program_id(2)
    tq = DIL_TILE
    band = DIL_BAND
    C = HEAD_DIM
    assert LANES == 2 * C
    kwin[0:tq, :] = kp_ref[0, 0]
    kwin[tq:2 * tq, :] = kc_ref[0, 0]
    kwin[2 * tq:3 * tq, :] = kn_ref[0, 0]
    vwin[0:tq, :] = vp_ref[0, 0]
    vwin[tq:2 * tq, :] = vc_ref[0, 0]
    vwin[2 * tq:3 * tq, :] = vn_ref[0, 0]

    def rows(start, size, stride):
        return pl.ds(start, size) if stride == 1 else pl.ds(start, size, stride=stride)

    for pi, (window, dil) in enumerate(DIL_PATTERNS):
        assert window // (2 * dil) == band and band * dil <= tq
        qb = min(2 * band, tq // dil)
        nk = qb + 2 * band
        qa = lax.broadcasted_iota(jnp.int32, (qb, nk), 0)
        kj = lax.broadcasted_iota(jnp.int32, (qb, nk), 1)
        band_bias = jnp.where((kj >= qa) & (kj <= qa + 2 * band), 0.0, NEG)
        kcol = lax.broadcasted_iota(jnp.int32, (1, nk), 1)
        vlane = lax.broadcasted_iota(jnp.int32, (nk, LANES), 1)
        qlane = lax.broadcasted_iota(jnp.int32, (qb, LANES), 1)
        one = jnp.ones((nk, LANES), BF16)
        zero = jnp.zeros((nk, LANES), BF16)
        units = [(c, u) for c in range(dil) for u in range(tq // (dil * qb))]
        group = 4
        first = qlane < C
        for g0 in range(0, len(units), group):
            scores, values = [], []
            for c, u in units[g0:g0 + group]:
                q0 = c + dil * u * qb
                k0 = tq + q0 - dil * band
                pos = i * tq + (q0 - dil * band) + dil * kcol
                bias = band_bias + jnp.where((pos >= 0) & (pos < seq), 0.0, NEG)
                qr = q_ref[0, 0, rows(q0, qb, dil), :].astype(BF16)
                kr = kwin[rows(k0, nk, dil), :].astype(BF16)
                values.append(vwin[rows(k0, nk, dil), :].astype(BF16))
                if nk % LANES == 0:
                    k_bd = jnp.concatenate([jnp.where(vlane < C, kr, zero), jnp.where(vlane >= C, kr, zero)], axis=0)
                    s2 = _dot_nt(qr, k_bd)
                    scores.append(s2[:, 0:nk] + bias)
                    scores.append(s2[:, nk:2 * nk] + bias)
                else:
                    for hh in range(LANES // C):
                        sl = slice(hh * C, (hh + 1) * C)
                        scores.append(_dot_nt(qr[:, sl], kr[:, sl]) + bias)
            s = jnp.concatenate(scores, axis=0)
            m = jnp.max(s, axis=-1, keepdims=True)
            pb = jnp.exp(s - m).astype(BF16)
            for g, (c, u) in enumerate(units[g0:g0 + group]):
                b0 = slice(2 * g * qb, (2 * g + 1) * qb)
                b1 = slice((2 * g + 1) * qb, (2 * g + 2) * qb)
                pvl0 = _dot(pb[b0], jnp.where(vlane < C, values[g], one))
                pvl1 = _dot(pb[b1], jnp.where(vlane >= C, values[g], one))
                sel = rows(c + dil * u * qb, qb, dil)
                m_st[pi, sel, :] = jnp.where(first, m[b0], m[b1])
                l_st[pi, sel, :] = pltpu.roll(jnp.where(first, pvl1, pvl0), C, 1)
                acc_st[pi, sel, :] = jnp.where(first, pvl0, pvl1)
    m_all = m_st[0]
    for pi in range(1, len(DIL_PATTERNS)):
        m_all = jnp.maximum(m_all, m_st[pi])
    num = jnp.zeros((tq, LANES), F32)
    den = jnp.zeros((tq, LANES), F32)
    for pi in range(len(DIL_PATTERNS)):
        w = jnp.exp(m_st[pi] - m_all)
        num = num + w * acc_st[pi]
        den = den + w * l_st[pi]
    o_ref[0, 0] = num / den


def _dilated_attention(q, k, v):
    _, B, S, _ = q.shape
    t = DIL_TILE
    nq = S // t
    cur = pl.BlockSpec((1, 1, t, LANES), lambda b, p, i: (p, b, i, 0))
    prev = pl.BlockSpec((1, 1, t, LANES), lambda b, p, i: (p, b, jnp.maximum(i - 1, 0), 0))
    nxt = pl.BlockSpec((1, 1, t, LANES), lambda b, p, i: (p, b, jnp.minimum(i + 1, nq - 1), 0))
    return pl.pallas_call(
        functools.partial(_dil_kernel, seq=S), grid=(B, 2, nq),
        in_specs=[cur, prev, cur, nxt, prev, cur, nxt],
        out_specs=cur, out_shape=jax.ShapeDtypeStruct(q.shape, F32),
        scratch_shapes=[pltpu.VMEM((3 * t, LANES), F32), pltpu.VMEM((3 * t, LANES), F32),
                        pltpu.VMEM((3, t, LANES), F32), pltpu.VMEM((3, t, LANES), F32), pltpu.VMEM((3, t, LANES), F32)],
        compiler_params=_params(("parallel", "parallel", "parallel")), name="dilated_attn",
    )(q, k, k, k, v, v, v)


def _dot2(a, b):
    a_hi, a_lo = _split2(a)
    b_hi, b_lo = _split2(b)
    return _dot(a_hi, b_hi) + _dot(a_hi, b_lo) + _dot(a_lo, b_hi)


def _exact_rhs_dot(x, m):
    hi, mid, lo = _split3(x)
    return _dot(hi, m) + _dot(mid, m) + _dot(lo, m)


def _ssm_ops_kernel(lamr_row, lami_row, ldt_row, lamr_col, lami_col, ldt_col, brt_ref, bit_ref, crt_ref, cit_ref,
                    t_ref, p_ref, q_ref, al_ref):
    L, N, P = SSM_CHUNK, SSM_STATE, SSM_GROUP
    K = L * P
    lane_j = lax.broadcasted_iota(jnp.int32, (L, K), 1) // P
    row_j = lax.broadcasted_iota(jnp.int32, (L, K), 0)
    sel_r = lax.broadcasted_iota(jnp.int32, (K, L), 0) // P
    sel_j = lax.broadcasted_iota(jnp.int32, (K, L), 1)
    jcol = lax.broadcasted_iota(jnp.int32, (N, L), 1).astype(F32)
    jrow = lax.broadcasted_iota(jnp.int32, (L, N), 0).astype(F32)
    kcat, p_parts, q_parts, al_parts = [], [], [], []
    for d in range(2):
        expand = jnp.where(row_j == (lane_j if d == 0 else L - 1 - lane_j), 1.0, 0.0).astype(BF16)
        select = jnp.where(sel_j == (L - 1 - sel_r if d == 0 else sel_r), 1.0, 0.0).astype(BF16)
        lr_c, li_c, dt_c = lamr_col[d, 0], lami_col[d, 0], jnp.exp(ldt_col[d, 0])
        pmag = jnp.exp(jcol * (lr_c * dt_c))
        pr0 = _exact_rhs_dot(pmag * jnp.cos(jcol * (li_c * dt_c)), expand)
        pi0 = _exact_rhs_dot(pmag * jnp.sin(jcol * (li_c * dt_c)), expand)
        mag_c = jnp.exp(lr_c * dt_c)
        ar_c, ai_c = mag_c * jnp.cos(li_c * dt_c), mag_c * jnp.sin(li_c * dt_c)
        pr1, pi1 = ar_c * pr0 - ai_c * pi0, ar_c * pi0 + ai_c * pr0
        cr = jnp.concatenate([crt_ref[d, 0]] * (K // LANES), axis=1)
        ci = jnp.concatenate([cit_ref[d, 0]] * (K // LANES), axis=1)
        lr_r, li_r, dt_r = lamr_row[d, 0], lami_row[d, 0], jnp.exp(ldt_row[d, 0])
        mag_r = jnp.exp(lr_r * dt_r)
        ar_r, ai_r = mag_r * jnp.cos(li_r * dt_r), mag_r * jnp.sin(li_r * dt_r)
        den = lr_r * lr_r + li_r * li_r
        fr = ((ar_r - 1.0) * lr_r + ai_r * li_r) / den
        fi = (ai_r * lr_r - (ar_r - 1.0) * li_r) / den
        bbr = fr * brt_ref[d, 0] - fi * bit_ref[d, 0]
        bbi = fr * bit_ref[d, 0] + fi * brt_ref[d, 0]
        kcat.append(_dot2(bbr, pr0 * cr - pi0 * ci) - _dot2(bbi, pr0 * ci + pi0 * cr))
        q_parts.append((pr1 * cr - pi1 * ci, -(pi1 * cr + pr1 * ci)))
        rmag = jnp.exp(jrow * (lr_r * dt_r))
        prr = rmag * jnp.cos(jrow * (li_r * dt_r))
        pir = rmag * jnp.sin(jrow * (li_r * dt_r))
        prs, pis = _exact_lhs_dot(select, prr), _exact_lhs_dot(select, pir)
        bbr_t = jnp.concatenate([bbr] * L, axis=0)
        bbi_t = jnp.concatenate([bbi] * L, axis=0)
        p_parts.append((prs * bbr_t - pis * bbi_t, prs * bbi_t + pis * bbr_t))
        al_parts.append((ar_r * prr[L - 1:L] - ai_r * pir[L - 1:L], ar_r * pir[L - 1:L] + ai_r * prr[L - 1:L]))
    zeros = jnp.zeros((P, K), F32)
    ksum = (jnp.concatenate([zeros, kcat[0]], axis=1)
            + pltpu.roll(jnp.concatenate([kcat[1], zeros], axis=1), P, 1))
    per_tile = LANES // P
    for b in range(per_tile):
        rolled = ksum if b == 0 else pltpu.roll(ksum, P * b, 1)
        for a in range(L // per_tile):
            r = per_tile * a + b
            t_ref[0, P * r:P * (r + 1), :] = rolled[:, K - LANES * a:2 * K - LANES * a].astype(BF16)
    p_ref[0] = jnp.concatenate([p_parts[0][0], p_parts[1][0], p_parts[0][1], p_parts[1][1]], axis=1).astype(BF16)
    q_ref[0] = jnp.concatenate([q_parts[0][0], q_parts[1][0], q_parts[0][1], q_parts[1][1]], axis=0).astype(BF16)
    al_ref[0] = jnp.concatenate([jnp.concatenate([al_parts[0][0], al_parts[1][0]], axis=1),
                                 jnp.concatenate([al_parts[0][1], al_parts[1][1]], axis=1)], axis=0)


def _ssm_operators(lam_re, lam_im, log_dt, b_re, b_im, c_re, c_im):
    L, G, N, P = SSM_CHUNK, SSM_GROUPS, SSM_STATE, SSM_GROUP
    K = L * P
    rowf = lambda v: v.reshape(2, G, 1, N)
    colf = lambda v: v.reshape(2, G, N, 1)
    ldt = jnp.broadcast_to(log_dt[:, :, None], (2, G, N))
    tr = lambda v: jnp.swapaxes(v, 2, 3)
    ct = lambda v: jnp.tile(tr(v), (1, 1, 1, LANES // P))
    spec = lambda shape: pl.BlockSpec((2, 1) + shape, lambda g: (0, g, 0, 0))
    t_sum, p_cat, q_cat, al = pl.pallas_call(
        _ssm_ops_kernel, grid=(G,),
        in_specs=[spec((1, N))] * 3 + [spec((N, 1))] * 3 + [spec((P, N))] * 2 + [spec((N, LANES))] * 2,
        out_specs=[pl.BlockSpec((1, K, K), lambda g: (g, 0, 0)), pl.BlockSpec((1, K, 4 * N), lambda g: (g, 0, 0)),
                   pl.BlockSpec((1, 4 * N, K), lambda g: (g, 0, 0)), pl.BlockSpec((1, 2, 2 * N), lambda g: (g, 0, 0))],
        out_shape=[jax.ShapeDtypeStruct((G, K, K), BF16), jax.ShapeDtypeStruct((G, K, 4 * N), BF16),
                   jax.ShapeDtypeStruct((G, 4 * N, K), BF16), jax.ShapeDtypeStruct((G, 2, 2 * N), F32)],
        compiler_params=_params(("parallel",)), name="ssm_ops",
    )(rowf(lam_re), rowf(lam_im), rowf(ldt), colf(lam_re), colf(lam_im), colf(ldt),
      tr(b_re), tr(b_im), ct(c_re), ct(c_im))
    return t_sum, p_cat, q_cat, al[:, 0, :].reshape(1, G * 2 * N), al[:, 1, :].reshape(1, G * 2 * N)


def _ssm_state_in_kernel(u_ref, p_ref, xre_ref, xim_ref):
    x = _dot(u_ref[0].astype(BF16), p_ref[0])
    xre_ref[...] = x[:, 0:LANES]
    xim_ref[...] = x[:, LANES:2 * LANES]


def _ssm_state_in(u_t, p_cat):
    G, M, K = u_t.shape
    tm = min(M, 256)
    return pl.pallas_call(
        _ssm_state_in_kernel, grid=(G, M // tm),
        in_specs=[pl.BlockSpec((1, tm, K), lambda g, m: (g, m, 0)),
                  pl.BlockSpec((1, K, 2 * LANES), lambda g, m: (g, 0, 0))],
        out_specs=[pl.BlockSpec((tm, LANES), lambda g, m: (m, g))] * 2,
        out_shape=[jax.ShapeDtypeStruct((M, G * LANES), F32)] * 2,
        compiler_params=_params(("parallel", "parallel")), name="ssm_state_in",
    )(u_t, p_cat)


def _ssm_scan_kernel(xre_ref, xim_ref, are_ref, aim_ref, hfre_ref, hfim_ref, hbre_ref, hbim_ref):
    nc = xre_ref.shape[1]
    width = xre_ref.shape[2]
    ar = are_ref[...]
    ai = aim_ref[...]
    lane = lax.broadcasted_iota(jnp.int32, (1, width), 1)
    is_fwd = (lane % LANES) < SSM_STATE

    def step(i, carry):
        hr, hi = carry
        ib = nc - 1 - i
        hfre_ref[0, pl.ds(i, 1), :] = hr
        hfim_ref[0, pl.ds(i, 1), :] = hi
        hbre_ref[0, pl.ds(ib, 1), :] = hr
        hbim_ref[0, pl.ds(ib, 1), :] = hi
        xr = jnp.where(is_fwd, xre_ref[0, pl.ds(i, 1), :], xre_ref[0, pl.ds(ib, 1), :])
        xi = jnp.where(is_fwd, xim_ref[0, pl.ds(i, 1), :], xim_ref[0, pl.ds(ib, 1), :])
        return ar * hr - ai * hi + xr, ar * hi + ai * hr + xi

    zero = jnp.zeros((1, width), F32)
    lax.fori_loop(0, nc, step, (zero, zero))


def _ssm_scan(xre, xim, al_re, al_im):
    B, nc, width = xre.shape
    blk = pl.BlockSpec((1, nc, width), lambda b: (b, 0, 0))
    vec = pl.BlockSpec((1, width), lambda b: (0, 0))
    return pl.pallas_call(
        _ssm_scan_kernel, grid=(B,),
        in_specs=[blk, blk, vec, vec], out_specs=[blk] * 4,
        out_shape=[jax.ShapeDtypeStruct((B, nc, width), F32)] * 4,
        compiler_params=_params(("parallel",)), name="ssm_scan",
    )(xre, xim, al_re, al_im)


def _ssm_out_kernel(u_ref, t_ref, hfre_ref, hfim_ref, hbre_ref, hbim_ref, q_ref, y_ref):
    tm = u_ref.shape[1]
    lane = lax.broadcasted_iota(jnp.int32, (tm, LANES), 1)
    is_fwd = lane < SSM_STATE
    h_re = jnp.where(is_fwd, hfre_ref[...], hbre_ref[...])
    h_im = jnp.where(is_fwd, hfim_ref[...], hbim_ref[...])
    h_cat = jnp.concatenate([h_re, h_im], axis=1).astype(BF16)
    y_ref[0] = _dot(u_ref[0].astype(BF16), t_ref[0]) + _dot(h_cat, q_ref[0])


def _ssm_out(u_t, t_sum, hs, q_cat):
    G, M, K = u_t.shape
    tm = min(M, 256)
    hblk = pl.BlockSpec((tm, LANES), lambda g, m: (m, g))
    return pl.pallas_call(
        _ssm_out_kernel, grid=(G, M // tm),
        in_specs=[pl.BlockSpec((1, tm, K), lambda g, m: (g, m, 0)),
                  pl.BlockSpec((1, K, K), lambda g, m: (g, 0, 0)),
                  hblk, hblk, hblk, hblk,
                  pl.BlockSpec((1, 2 * LANES, K), lambda g, m: (g, 0, 0))],
        out_specs=pl.BlockSpec((1, tm, K), lambda g, m: (g, m, 0)),
        out_shape=jax.ShapeDtypeStruct((G, M, K), F32),
        compiler_params=_params(("parallel", "parallel")), name="ssm_out",
    )(u_t, t_sum, *hs, q_cat)


def _s5_scan(u_t, ops, batch):
    t_sum, p_cat, q_cat, al_re, al_im = ops
    G, M, _ = u_t.shape
    nc = M // batch
    xre, xim = _ssm_state_in(u_t, p_cat)
    width = G * LANES
    hs = _ssm_scan(xre.reshape(batch, nc, width), xim.reshape(batch, nc, width), al_re, al_im)
    return _ssm_out(u_t, t_sum, tuple(h.reshape(M, width) for h in hs), q_cat)


def _na_bias_kernel(rpb_ref, out_ref):
    h = pl.program_id(0)
    n_dr, n_dc = 2 * NA_WIN_R - 1, 2 * NA_WIN_C - 1
    qc = lax.broadcasted_iota(jnp.int32, (GRID_W, GRID_W), 0)
    kc = lax.broadcasted_iota(jnp.int32, (GRID_W, GRID_W), 1)
    dc = jnp.clip(kc - qc + (NA_WIN_C - 1), 0, n_dc - 1)
    cstart = jnp.clip(qc - NA_WIN_C // 2, 0, GRID_W - NA_WIN_C)
    col_ok = (kc >= cstart) & (kc < cstart + NA_WIN_C)
    base = []
    for dr in range(n_dr):
        acc = jnp.zeros((GRID_W, GRID_W), F32)
        for d in range(n_dc):
            acc = jnp.where(dc == d, rpb_ref[(h * n_dr + dr) * n_dc + d], acc)
        base.append(jnp.where(col_ok, acc, NEG))
    for off in range(NA_WIN_R):
        out_ref[0, off] = jnp.concatenate([base[w - off + NA_WIN_R - 1] for w in range(NA_WIN_R)], axis=1)


def _na_bias_table(rpb):
    return pl.pallas_call(
        _na_bias_kernel, grid=(HEADS,),
        in_specs=[pl.BlockSpec(memory_space=pltpu.SMEM)],
        out_specs=pl.BlockSpec((1, NA_WIN_R, GRID_W, NA_WIN_R * GRID_W), lambda h: (h, 0, 0, 0)),
        out_shape=jax.ShapeDtypeStruct((HEADS, NA_WIN_R, GRID_W, NA_WIN_R * GRID_W), F32),
        compiler_params=_params(("parallel",)), name="na_bias",
    )(rpb.astype(F32).reshape(-1))


def _na_kernel(q_ref, kp_ref, kc_ref, kn_ref, vp_ref, vc_ref, vn_ref, bias_ref, o_ref, kbuf, vbuf, *, rows):
    rb = pl.program_id(1)
    Wg = GRID_W
    C = HEAD_DIM
    halo = NA_HALO_ROWS * Wg
    cur = NA_ROWS_PER_STEP * Wg
    win = NA_WIN_R * Wg
    kbuf[0:halo, :] = kp_ref[0]
    kbuf[halo:halo + cur, :] = kc_ref[0]
    kbuf[halo + cur:2 * halo + cur, :] = kn_ref[0]
    vbuf[0:halo, :] = vp_ref[0]
    vbuf[halo:halo + cur, :] = vc_ref[0]
    vbuf[halo + cur:2 * halo + cur, :] = vn_ref[0]
    r0 = rb * NA_ROWS_PER_STEP
    pairs = HEADS // 2
    assert LANES == 2 * C
    klane = lax.broadcasted_iota(jnp.int32, (win, LANES), 1)
    olane = lax.broadcasted_iota(jnp.int32, (Wg, LANES), 1)
    zero = jnp.zeros((win, LANES), BF16)
    one = jnp.ones((win, LANES), BF16)
    group = 2
    for j0 in range(0, NA_ROWS_PER_STEP, group):
        scores, values = [], []
        for j in range(j0, j0 + group):
            r = r0 + j
            rs = jnp.clip(r - NA_WIN_R // 2, 0, rows - NA_WIN_R)
            off = r - rs
            start = pl.multiple_of((rs - r0 + NA_HALO_ROWS) * Wg, Wg)
            kw = kbuf[pl.ds(start, win), :]
            values.append(vbuf[pl.ds(start, win), :])
            for pr in range(pairs):
                tile = slice(pr * LANES, (pr + 1) * LANES)
                k_bd = jnp.concatenate([jnp.where(klane < C, kw[:, tile], zero),
                                        jnp.where(klane >= C, kw[:, tile], zero)], axis=0)
                s2 = _dot_nt(q_ref[0, j * Wg:(j + 1) * Wg, tile], k_bd)
                scores.append(s2[:, 0:win] + bias_ref[2 * pr, off])
                scores.append(s2[:, win:2 * win] + bias_ref[2 * pr + 1, off])
        s = jnp.concatenate(scores, axis=0)
        pb = jnp.exp(s - jnp.max(s, axis=-1, keepdims=True)).astype(BF16)
        for g in range(group):
            outs = []
            for pr in range(pairs):
                tile = slice(pr * LANES, (pr + 1) * LANES)
                b0 = slice((g * HEADS + 2 * pr) * Wg, (g * HEADS + 2 * pr + 1) * Wg)
                b1 = slice((g * HEADS + 2 * pr + 1) * Wg, (g * HEADS + 2 * pr + 2) * Wg)
                pvl0 = _dot(pb[b0], jnp.where(klane < C, values[g][:, tile], one))
                pvl1 = _dot(pb[b1], jnp.where(klane >= C, values[g][:, tile], one))
                first = olane < C
                outs.append(jnp.where(first, pvl0, pvl1) / pltpu.roll(jnp.where(first, pvl1, pvl0), C, 1))
            o_ref[0, (j0 + g) * Wg:(j0 + g + 1) * Wg, :] = jnp.concatenate(outs, axis=1)


def _neighbourhood_attention(q, k, v, bias):
    B, S, W = q.shape
    rows = S // GRID_W
    cur = NA_ROWS_PER_STEP * GRID_W
    halo = NA_HALO_ROWS * GRID_W
    nrb = S // cur
    per = cur // halo
    nh = S // halo
    blk = pl.BlockSpec((1, cur, W), lambda b, i: (b, i, 0))
    prev = pl.BlockSpec((1, halo, W), lambda b, i: (b, jnp.maximum(i * per - 1, 0), 0))
    nxt = pl.BlockSpec((1, halo, W), lambda b, i: (b, jnp.minimum((i + 1) * per, nh - 1), 0))
    bspec = pl.BlockSpec(bias.shape, lambda b, i: (0, 0, 0, 0))
    return pl.pallas_call(
        functools.partial(_na_kernel, rows=rows), grid=(B, nrb),
        in_specs=[blk, prev, blk, nxt, prev, blk, nxt, bspec],
        out_specs=blk, out_shape=jax.ShapeDtypeStruct((B, S, W), F32),
        scratch_shapes=[pltpu.VMEM((cur + 2 * halo, W), BF16), pltpu.VMEM((cur + 2 * halo, W), BF16)],
        compiler_params=_params(("parallel", "parallel")), name="neighbourhood_attn",
    )(q, k, k, k, v, v, v, bias)


def _post_kernel(x_ref, of_ref, ob_ref, gate_ref, yb_ref, yt_ref, u_ref, yd_ref,
                 dnw_ref, bd_ref, dskip_ref, gluw_ref, glub_ref, wout_ref, ln2_ref, w1_ref, w2_ref, fnw_ref,
                 out_ref, ybuf, *, final):
    x = x_ref[...]
    o = of_ref[...] + ob_ref[...]
    ms = _head_sums(o * o, bd_ref[...]) * (1.0 / HEAD_DIM)
    g = gate_ref[...]
    y_a = o * lax.rsqrt(ms + EPS) * dnw_ref[...] * (g * _sigmoid(g))
    y = _chunk_unflatten(yt_ref, ybuf) + dskip_ref[...] * u_ref[...]
    z = 0.5 * y * (1.0 + jnp.tanh(math.sqrt(2.0 / math.pi) * (y + 0.044715 * (y * y * y))))
    y_c = z * _sigmoid(_dot(z.astype(BF16), gluw_ref[...]) + glub_ref[...])
    mix = jnp.concatenate([y_a, yb_ref[0], yb_ref[1], y_c, yd_ref[...]], axis=1).astype(BF16)
    x1 = x + _dot(mix, wout_ref[...])
    hb = _rms(x1, ln2_ref[...]).astype(BF16)
    ffc = D_MODEL
    acc = x1
    for c in range(D_FF // ffc):
        a = jnp.maximum(_dot(hb, w1_ref[:, c * ffc:(c + 1) * ffc]), 0.0)
        acc = acc + _dot((a * a).astype(BF16), w2_ref[c * ffc:(c + 1) * ffc, :])
    if final:
        acc = _rms(acc, fnw_ref[...])
    out_ref[...] = acc


def _post(x, o_f, o_b, gate, y_b, y_t, u, y_d, wts, final):
    T = x.shape[0]
    tm = min(TOKEN_TILE, T)
    W = WIDTH
    row = lambda n: pl.BlockSpec((tm, n), lambda i: (i, 0))
    const = lambda a: pl.BlockSpec(a.shape, lambda i: (0, 0), pipeline_mode=pl.Buffered(1))
    return pl.pallas_call(
        functools.partial(_post_kernel, final=final), grid=(T // tm,),
        in_specs=([row(D_MODEL)] + [row(W)] * 3 + [pl.BlockSpec((2, tm, LANES), lambda i: (0, i, 0)),
                                                   pl.BlockSpec((SSM_GROUPS, tm // SSM_CHUNK, y_t.shape[2]),
                                                                lambda i: (0, i, 0))] + [row(W)] * 2
                  + [const(a) for a in wts]),
        out_specs=row(D_MODEL), out_shape=jax.ShapeDtypeStruct((T, D_MODEL), F32),
        scratch_shapes=[pltpu.VMEM((W // LANES, tm, LANES), F32)],
        compiler_params=_params(("parallel",)), name="post",
    )(x, o_f, o_b, gate, y_b, y_t, u, y_d, *wts)


def _rope_tables(seq):
    inv_freq = ROPE_THETA ** (-jnp.arange(0, HEAD_DIM, 2, dtype=F32) / HEAD_DIM)
    ang = jnp.arange(seq, dtype=F32)[:, None] * inv_freq[None, :]
    cos, sin = jnp.cos(ang), jnp.sin(ang)
    cos_t = jnp.tile(jnp.concatenate([cos, cos], axis=1), (1, HEADS))
    sin_t = jnp.tile(jnp.concatenate([-sin, sin], axis=1), (1, HEADS))
    return cos_t, sin_t


def _layer_weights(i, p):
    W = WIDTH
    w = p['w_in'][i]
    o_gate, o_a, o_b, o_dil, o_ssm, o_na = 3 * W, 4 * W, 4 * W + 2 * HEADS, 4 * W + 4 * HEADS, 7 * W + 4 * HEADS, 8 * W + 4 * HEADS
    pad = jnp.zeros((D_MODEL, LANES - 4 * HEADS), F32)
    w_in = jnp.concatenate([w[:, 0:o_a], w[:, o_dil:o_na + 3 * W], w[:, o_a:o_dil], pad], axis=1).astype(BF16)
    assert w_in.shape[1] == IN_COLS_PADDED
    lane_pad = lambda v: jnp.concatenate([v.reshape(1, -1).astype(F32), jnp.zeros((1, LANES - v.size), F32)], axis=1)
    idx = np.arange(W)
    bd = jnp.asarray((idx[:, None] // HEAD_DIM) == (idx[None, :] // HEAD_DIM), BF16)
    row = lambda v: v.reshape(1, -1).astype(F32)
    return dict(
        ln1=row(p['ln1_w'][i]), w_in=w_in,
        conv_w=p['dn_conv_w'][i].astype(F32), alog=lane_pad(p['dn_a_log'][i]), dtb=lane_pad(p['dn_dt_bias'][i]),
        bd=bd,
        ssm=_ssm_operators(*(p[n][i].astype(F32) for n in
                             ('ssm_lam_re', 'ssm_lam_im', 'ssm_log_dt', 'ssm_b_re', 'ssm_b_im', 'ssm_c_re', 'ssm_c_im'))),
        na_bias=_na_bias_table(p['na_rpb'][i]),
        post=(row(jnp.tile(p['dn_norm_w'][i], HEADS)), bd, row(p['ssm_d'][i]), p['ssm_glu_w'][i].astype(BF16),
              row(p['ssm_glu_b'][i]), p['w_out'][i].astype(BF16), row(p['ln2_w'][i]),
              p['w_ff1'][i].astype(BF16), p['w_ff2'][i].astype(BF16), row(p['final_norm_w'])),
    )


def _trunk(x, layers):
    B, S, D = x.shape
    T = B * S
    cos_t, sin_t = _rope_tables(S)
    xt = x.reshape(T, D)
    seq3 = lambda a: a.reshape(B, S, a.shape[-1])
    flat = lambda a: a.reshape(T, a.shape[-1])
    for i, lw in enumerate(layers):
        q, k, v, gb, z_gate, dq, dk, dv, z_ssm, u_t, nq, nk, nv = _in_proj(
            xt, lw['ln1'], lw['w_in'], cos_t, sin_t, lw['conv_w'], lw['alog'], lw['dtb'], lw['bd'], S)
        o_f, o_b = _dn_scan(seq3(q), seq3(k), seq3(v), seq3(gb))
        pairs = lambda a: a.reshape(2, B, S, LANES)
        y_b = _dilated_attention(pairs(dq), pairs(dk), pairs(dv)).reshape(2, T, LANES)
        y_t = _s5_scan(u_t, lw['ssm'], B)
        y_d = _neighbourhood_attention(seq3(nq), seq3(nk), seq3(nv), lw['na_bias'])
        xt = _post(xt, flat(o_f), flat(o_b), z_gate, y_b, y_t, z_ssm, flat(y_d),
                   lw['post'], final=(i == len(layers) - 1))
    return xt.reshape(B, S, D)


def kernel(x_prompt, x_sample, ln1_w, w_in, dn_conv_w, dn_a_log, dn_dt_bias, dn_norm_w, ssm_lam_re, ssm_lam_im,
           ssm_log_dt, ssm_b_re, ssm_b_im, ssm_c_re, ssm_c_im, ssm_d, ssm_glu_w, ssm_glu_b, na_rpb, w_out,
           ln2_w, w_ff1, w_ff2, final_norm_w):
    p = dict(ln1_w=ln1_w, w_in=w_in, dn_conv_w=dn_conv_w, dn_a_log=dn_a_log, dn_dt_bias=dn_dt_bias,
             dn_norm_w=dn_norm_w, ssm_lam_re=ssm_lam_re, ssm_lam_im=ssm_lam_im, ssm_log_dt=ssm_log_dt,
             ssm_b_re=ssm_b_re, ssm_b_im=ssm_b_im, ssm_c_re=ssm_c_re, ssm_c_im=ssm_c_im, ssm_d=ssm_d,
             ssm_glu_w=ssm_glu_w, ssm_glu_b=ssm_glu_b, na_rpb=na_rpb, w_out=w_out, ln2_w=ln2_w,
             w_ff1=w_ff1, w_ff2=w_ff2, final_norm_w=final_norm_w)
    layers = [_layer_weights(i, p) for i in range(DEPTH)]
    return (_trunk(x_prompt, layers), _trunk(x_sample, layers))
```

```python
import functools
import math

import numpy as np
import jax
import jax.numpy as jnp
from jax import lax
from jax.experimental import pallas as pl
from jax.experimental.pallas import tpu as pltpu

F32 = jnp.float32
BF16 = jnp.bfloat16

D_MODEL = 1024
DEPTH = 2
HEAD_DIM = 64
HEADS = 4
WIDTH = HEADS * HEAD_DIM
DN_CHUNK = 64
CONV_WIDTH = 5
DIL_PATTERNS = ((128, 1), (512, 4), (2048, 16))
SSM_GROUP = 16
SSM_GROUPS = 16
SSM_STATE = 64
SSM_CHUNK = 64
GRID_W = 64
NA_WIN_R = 8
NA_WIN_C = 16
D_FF = 4 * D_MODEL
ROPE_THETA = 10000.0
EPS = 1e-6
NEG = -1e30

LANES = 128
TOKEN_TILE = 512
DN_BLOCK = 256
DIL_TILE = 1024
DIL_BAND = 64
NA_ROWS_PER_STEP = 8
NA_HALO_ROWS = 4
IN_COLS_PADDED = 23 * LANES
VMEM_LIMIT = 56 * 2**20


def _params(sem):
    return pltpu.CompilerParams(dimension_semantics=sem, vmem_limit_bytes=VMEM_LIMIT)


def _dot(a, b):
    return jnp.dot(a, b, preferred_element_type=F32)


def _dot_nt(a, b):
    return lax.dot_general(a, b, (((1,), (1,)), ((), ())), preferred_element_type=F32)


def _bmm(a, b):
    return lax.dot_general(a, b, (((2,), (1,)), ((0,), (0,))), preferred_element_type=F32)


def _bmm_nt(a, b):
    return lax.dot_general(a, b, (((2,), (2,)), ((0,), (0,))), preferred_element_type=F32)


def _split2(x):
    hi = x.astype(BF16)
    lo = (x - hi.astype(F32)).astype(BF16)
    return hi, lo


def _split3(x):
    hi = x.astype(BF16)
    r = x - hi.astype(F32)
    mid = r.astype(BF16)
    lo = (r - mid.astype(F32)).astype(BF16)
    return hi, mid, lo


def _exact_lhs_dot(m, x):
    hi, mid, lo = _split3(x)
    return _dot(m, hi) + _dot(m, mid) + _dot(m, lo)


def _head_sums(sq, bd):
    hi, lo = _split2(sq)
    return _dot(hi, bd) + _dot(lo, bd)


def _sigmoid(x):
    return 1.0 / (1.0 + jnp.exp(-x))


def _rms(x, w):
    ms = jnp.mean(x * x, axis=-1, keepdims=True)
    return x * lax.rsqrt(ms + EPS) * w


def _group_masks(rows):
    lane_group = lax.broadcasted_iota(jnp.int32, (rows, LANES), 1) // SSM_GROUP
    return [lane_group == b for b in range(LANES // SSM_GROUP)]


def _chunk_flatten(zs, sbuf, ut_ref):
    L, P = SSM_CHUNK, SSM_GROUP
    per = LANES // P
    nck = zs.shape[0] // L
    masks = _group_masks(nck)
    for h in range(zs.shape[1] // LANES):
        sbuf[h] = zs[:, h * LANES:(h + 1) * LANES]
    for h in range(zs.shape[1] // LANES):
        out = [[jnp.zeros((nck, LANES), F32) for _ in range(L // per)] for _ in range(per)]
        for a in range(L // per):
            for b in range(per):
                v = sbuf[h, pl.ds(per * a + b, nck, stride=L), :]
                for d in range(per):
                    g = (b - d) % per
                    out[g][a] = jnp.where(masks[b], v if d == 0 else pltpu.roll(v, P * d, 1), out[g][a])
        for g in range(per):
            ut_ref[h * per + g] = jnp.concatenate(out[g], axis=1)


def _chunk_unflatten(yt_ref, ybuf):
    L, P = SSM_CHUNK, SSM_GROUP
    per = LANES // P
    nck = yt_ref.shape[1]
    assert nck == per
    masks = _group_masks(nck)
    halves = yt_ref.shape[0] // per
    out = []
    for h in range(halves):
        for a in range(L // per):
            rows = [jnp.zeros((nck, LANES), F32) for _ in range(per)]
            for g in range(per):
                blk = yt_ref[h * per + g, :, a * LANES:(a + 1) * LANES]
                for d in range(per):
                    b = (g - d) % per
                    rows[b] = jnp.where(masks[g], blk if d == 0 else pltpu.roll(blk, P * d, 1), rows[b])
            for b in range(per):
                ybuf[h, L * a + per * b:L * a + per * (b + 1), :] = rows[b]
        out.append(jnp.concatenate([ybuf[h, pl.ds(L * a + c, per, stride=per), :]
                                    for c in range(nck) for a in range(L // per)], axis=0))
    return jnp.concatenate(out, axis=1)


def _in_proj_kernel(x_ref, xp_ref, xn_ref, lnw_ref, w_ref, cos_ref, sin_ref, cw_ref, alog_ref, dtb_ref, bd_ref,
                    q_ref, k_ref, v_ref, gb_ref, gate_ref, dq_ref, dk_ref, dv_ref, ssm_ref, ut_ref, nq_ref, nk_ref, nv_ref,
                    buf, sbuf, *, per_seq):
    lnw = lnw_ref[...]
    z = _dot(_rms(x_ref[...], lnw).astype(BF16), w_ref[...])
    tm = z.shape[0]
    W = WIDTH
    gate_ref[...] = z[:, 3 * W:4 * W]
    cos = cos_ref[...]
    sin = sin_ref[...]
    lane = lax.broadcasted_iota(jnp.int32, (tm, W), 1)
    first_half = (lane % HEAD_DIM) < (HEAD_DIM // 2)

    def rope(t):
        rot = jnp.where(first_half, pltpu.roll(t, W - HEAD_DIM // 2, 1), pltpu.roll(t, HEAD_DIM // 2, 1))
        return t * cos + rot * sin

    scale = HEAD_DIM ** -0.5
    for ref, t in ((dq_ref, rope(z[:, 4 * W:5 * W]) * scale), (dk_ref, rope(z[:, 5 * W:6 * W])),
                   (dv_ref, z[:, 6 * W:7 * W])):
        ref[0] = t[:, 0:LANES]
        ref[1] = t[:, LANES:2 * LANES]
    zs = z[:, 7 * W:8 * W]
    ssm_ref[...] = zs
    _chunk_flatten(zs, sbuf, ut_ref)
    nq_ref[...] = (z[:, 8 * W:9 * W] * scale).astype(BF16)
    nk_ref[...] = z[:, 9 * W:10 * W].astype(BF16)
    nv_ref[...] = z[:, 10 * W:11 * W].astype(BF16)

    j = pl.program_id(0) % per_seq
    half = CONV_WIDTH // 2
    w_dn = w_ref[:, 0:3 * W]
    zp = _dot(_rms(xp_ref[...], lnw).astype(BF16), w_dn)
    zn = _dot(_rms(xn_ref[...], lnw).astype(BF16), w_dn)
    buf[0:8, :] = jnp.where(j > 0, zp, 0.0)
    buf[8:8 + tm, :] = z[:, 0:3 * W]
    buf[8 + tm:16 + tm, :] = jnp.where(j < per_seq - 1, zn, 0.0)
    cw = cw_ref[...]
    acc = buf[8 - half:8 - half + tm, :] * cw[0:1, :]
    for t in range(1, CONV_WIDTH):
        acc = acc + buf[8 - half + t:8 - half + t + tm, :] * cw[t:t + 1, :]
    y = acc * _sigmoid(acc)
    bd = bd_ref[...]
    q = y[:, 0:W]
    k = y[:, W:2 * W]
    q_ref[...] = q * lax.rsqrt(_head_sums(q * q, bd) + EPS) * scale
    k_ref[...] = k * lax.rsqrt(_head_sums(k * k, bd) + EPS)
    v_ref[...] = y[:, 2 * W:3 * W]
    ab = z[:, 11 * W:11 * W + LANES]
    t = ab + dtb_ref[...]
    softplus = jnp.maximum(t, 0.0) + jnp.log(1.0 + jnp.exp(-jnp.abs(t)))
    g = -jnp.exp(alog_ref[...]) * softplus
    lane_ab = lax.broadcasted_iota(jnp.int32, ab.shape, 1)
    gb_ref[...] = jnp.where(lane_ab < 2 * HEADS, g, _sigmoid(ab))


def _in_proj(x, lnw, w, cos_t, sin_t, conv_w, alog, dtb, bd, seq):
    T = x.shape[0]
    tm = min(TOKEN_TILE, seq)
    per_seq = seq // tm
    r8 = tm // 8
    W = WIDTH
    row = lambda n: pl.BlockSpec((tm, n), lambda i: (i, 0))
    const = lambda a: pl.BlockSpec(a.shape, lambda i: (0, 0))
    tab = pl.BlockSpec((tm, W), lambda i: (i % per_seq, 0))
    prev = pl.BlockSpec((8, D_MODEL), lambda i: (jnp.maximum(i * r8 - 1, 0), 0))
    nxt = pl.BlockSpec((8, D_MODEL), lambda i: (jnp.minimum((i + 1) * r8, T // 8 - 1), 0))
    pair = pl.BlockSpec((2, tm, LANES), lambda i: (0, i, 0))
    pair_shape = jax.ShapeDtypeStruct((2, T, LANES), F32)
    f32 = lambda n: jax.ShapeDtypeStruct((T, n), F32)
    bf16 = lambda n: jax.ShapeDtypeStruct((T, n), BF16)
    nck = tm // SSM_CHUNK
    ut_shape = jax.ShapeDtypeStruct((SSM_GROUPS, T // SSM_CHUNK, SSM_CHUNK * SSM_GROUP), F32)
    ut_spec = pl.BlockSpec((SSM_GROUPS, nck, SSM_CHUNK * SSM_GROUP), lambda i: (0, i, 0))
    out_shapes = [f32(W), f32(W), f32(W), f32(LANES), f32(W), pair_shape, pair_shape, pair_shape, f32(W), ut_shape,
                  bf16(W), bf16(W), bf16(W)]
    out_specs = [row(W), row(W), row(W), row(LANES), row(W), pair, pair, pair, row(W), ut_spec,
                 row(W), row(W), row(W)]
    return pl.pallas_call(
        functools.partial(_in_proj_kernel, per_seq=per_seq), grid=(T // tm,),
        in_specs=[row(D_MODEL), prev, nxt, const(lnw), const(w), tab, tab,
                  const(conv_w), const(alog), const(dtb), const(bd)],
        out_specs=out_specs, out_shape=out_shapes,
        scratch_shapes=[pltpu.VMEM((tm + 16, 3 * W), F32), pltpu.VMEM((W // LANES, tm, LANES), F32)],
        compiler_params=_params(("parallel",)), name="in_proj",
    )(x, x, x, lnw, w, cos_t, sin_t, conv_w, alog, dtb, bd)


def _dn_scan_kernel(qf_ref, kf_ref, vf_ref, gf_ref, qr_ref, kr_ref, vr_ref, gr_ref, of_ref, or_ref,
                    s_scr, u_scr, w_scr, qg_scr, at_scr, kdt_scr, *, nc):
    C = DN_CHUNK
    tb = nc * C
    H = HEADS
    assert LANES == 2 * C

    @pl.when(pl.program_id(1) == 0)
    def _():
        s_scr[...] = jnp.zeros_like(s_scr)

    r = lax.broadcasted_iota(jnp.int32, (tb, tb), 0)
    c = lax.broadcasted_iota(jnp.int32, (tb, tb), 1)
    same = (r // C) == (c // C)
    ri = lax.broadcasted_iota(jnp.int32, (LANES, LANES), 0)
    ci = lax.broadcasted_iota(jnp.int32, (LANES, LANES), 1)
    eye128 = jnp.where(ri == ci, 1.0, 0.0).astype(BF16)
    same_head = (ri < C) == (ci < C)
    lane = lax.broadcasted_iota(jnp.int32, (tb, LANES), 1)
    first = lane < C
    ii = lax.broadcasted_iota(jnp.int32, (tb, LANES), 0) % C
    jj = lane % C
    lane_c = lax.broadcasted_iota(jnp.int32, (1, LANES), 1)

    def block_diag(x):
        ln = lax.broadcasted_iota(jnp.int32, x.shape, x.ndim - 1)
        zero = jnp.zeros_like(x)
        return jnp.concatenate([jnp.where(ln < C, x, zero), jnp.where(ln >= C, x, zero)], axis=x.ndim - 2)

    pick = lambda a, col: jnp.where(first, a[:, col:col + 1], a[:, col + 1:col + 2])
    parts = {name: [] for name in ('q', 'k', 'v', 'gam', 'gtot', 'beta', 'grow', 'incl', 'strict')}
    gtots = []
    for d, (q_ref, k_ref, v_ref, gb_ref) in enumerate(((qf_ref, kf_ref, vf_ref, gf_ref),
                                                       (qr_ref, kr_ref, vr_ref, gr_ref))):
        rev = d == 1
        gb = gb_ref[0]
        cum = same & ((c >= r) if rev else (c <= r))
        gam = _exact_lhs_dot(jnp.where(cum, 1.0, 0.0).astype(BF16), gb)
        gam3 = gam.reshape(nc, C, LANES)
        last = gam3[:, 0:1, :] if rev else gam3[:, C - 1:C, :]
        gtot = jnp.broadcast_to(last, (nc, C, LANES)).reshape(tb, LANES)
        gtots.append(gtot)
        g_hi, g_mid, g_lo = _split3(gam)
        gam_t = _dot_nt(eye128, g_hi) + _dot_nt(eye128, g_mid) + _dot_nt(eye128, g_lo)
        for pr in range(H // 2):
            ch0 = d * H + 2 * pr
            tile = slice(pr * LANES, (pr + 1) * LANES)
            parts['q'].append(q_ref[0, :, tile])
            parts['k'].append(k_ref[0, :, tile])
            parts['v'].append(v_ref[0, :, tile])
            parts['gam'].append(pick(gam, ch0))
            parts['gtot'].append(pick(gtot, ch0))
            parts['beta'].append(pick(gb, 2 * H + ch0))
            parts['grow'].append(jnp.concatenate(
                [jnp.concatenate([jnp.broadcast_to(gam_t[ch0:ch0 + 1, cc * C:(cc + 1) * C], (C, C)),
                                  jnp.broadcast_to(gam_t[ch0 + 1:ch0 + 2, cc * C:(cc + 1) * C], (C, C))], axis=1)
                 for cc in range(nc)], axis=0))
            parts['incl'].append((jj >= ii) if rev else (jj <= ii))
            parts['strict'].append((jj > ii) if rev else (jj < ii))
    stk = {name: jnp.concatenate(vals, axis=0) for name, vals in parts.items()}
    slots = 2 * (H // 2)
    rows_all = slots * tb
    chunks = lambda x: x.astype(BF16).reshape(slots * nc, C, x.shape[-1])
    ii_all = jnp.concatenate([ii] * slots, axis=0)
    jj_all = jnp.concatenate([jj] * slots, axis=0)
    qp, kp, vp, gam_c, beta = stk['q'], stk['k'], stk['v'], stk['gam'], stk['beta']
    dec = jnp.exp(jnp.where(stk['incl'], gam_c - stk['grow'], NEG))
    kb = kp * beta
    k_bd = block_diag(chunks(kp))
    a_raw = _bmm_nt(chunks(kb), k_bd).reshape(rows_all, LANES)
    qk = _bmm_nt(chunks(qp), k_bd).reshape(rows_all, LANES)
    lm = jnp.where(stk['strict'], a_raw * dec, 0.0)
    attn = qk * dec
    t_inv = jnp.where(ii_all == jj_all, 1.0, 0.0) - jnp.where((ii_all // 2) == (jj_all // 2), lm, 0.0)
    n = 2
    while n < C:
        off = ((ii_all // (2 * n)) == (jj_all // (2 * n))) & ((ii_all // n) != (jj_all // n))
        t3 = chunks(t_inv)
        bt = _bmm(chunks(jnp.where(off, lm, 0.0)), block_diag(t3))
        t_inv = t_inv - _bmm(t3, block_diag(bt.astype(BF16))).reshape(rows_all, LANES)
        n *= 2
    eg = jnp.exp(gam_c)
    rhs = jnp.concatenate([block_diag(chunks(vp * beta)), block_diag(chunks(kb * eg))], axis=2)
    sol = _bmm(chunks(t_inv), rhs).reshape(rows_all, 2 * LANES)
    u_scr[...] = sol[:, 0:LANES].reshape(slots, tb, LANES)
    w_scr[...] = sol[:, LANES:2 * LANES].astype(BF16).reshape(slots, tb, LANES)
    qg_scr[...] = (qp * eg).astype(BF16).reshape(slots, tb, LANES)
    at_scr[...] = attn.astype(BF16).reshape(slots, tb, LANES)
    kd = (kp * jnp.exp(stk['gtot'] - gam_c)).astype(BF16)
    for slot in range(slots):
        kdt_scr[slot] = _dot_nt(eye128, kd[slot * tb:(slot + 1) * tb]).astype(BF16)

    for step in range(nc):
        rng = []
        for slot in range(slots):
            cc = nc - 1 - step if slot >= H // 2 else step
            rng.append((cc * C, (cc + 1) * C))
        take = lambda ref: jnp.stack([ref[slot, lo:hi, :] for slot, (lo, hi) in enumerate(rng)], axis=0)
        s = s_scr[...]
        sb = s.astype(BF16)
        v_new = take(u_scr) - _bmm(take(w_scr), sb)
        vb = v_new.astype(BF16)
        o = _bmm(take(qg_scr), sb) + _bmm(take(at_scr), block_diag(vb))
        gl = []
        for slot, (lo, hi) in enumerate(rng):
            d, pr = divmod(slot, H // 2)
            ch0 = d * H + 2 * pr
            o_ref = or_ref if d == 1 else of_ref
            o_ref[0, lo:hi, pr * LANES:(pr + 1) * LANES] = o[slot]
            gt = gtots[d]
            gl.append(jnp.exp(jnp.where(lane_c < C, gt[lo:lo + 1, ch0:ch0 + 1], gt[lo:lo + 1, ch0 + 1:ch0 + 2])))
        kdt = jnp.stack([kdt_scr[slot, :, lo:hi] for slot, (lo, hi) in enumerate(rng)], axis=0)
        upd = _bmm(kdt, vb)
        s_scr[...] = s * jnp.stack(gl, axis=0) + jnp.where(same_head[None], upd, 0.0)


def _dn_scan(q, k, v, gb):
    B, S, W = q.shape
    tb = min(DN_BLOCK, S)
    nb = S // tb
    nc = tb // DN_CHUNK
    fwd = lambda n: pl.BlockSpec((1, tb, n), lambda b, i: (b, i, 0))
    bwd = lambda n: pl.BlockSpec((1, tb, n), lambda b, i: (b, nb - 1 - i, 0))
    slots = HEADS
    return pl.pallas_call(
        functools.partial(_dn_scan_kernel, nc=nc), grid=(B, nb),
        in_specs=[fwd(W), fwd(W), fwd(W), fwd(LANES), bwd(W), bwd(W), bwd(W), bwd(LANES)],
        out_specs=[fwd(W), bwd(W)], out_shape=[jax.ShapeDtypeStruct((B, S, W), F32)] * 2,
        scratch_shapes=[pltpu.VMEM((slots, LANES, LANES), F32), pltpu.VMEM((slots, tb, LANES), F32),
                        pltpu.VMEM((slots, tb, LANES), BF16), pltpu.VMEM((slots, tb, LANES), BF16),
                        pltpu.VMEM((slots, tb, LANES), BF16), pltpu.VMEM((slots, LANES, tb), BF16)],
        compiler_params=_params(("parallel", "arbitrary")), name="dn_scan",
    )(q, k, v, gb, q, k, v, gb)


def _dil_kernel(q_ref, kp_ref, kc_ref, kn_ref, vp_ref, vc_ref, vn_ref, o_ref,
                kwin, vwin, m_st, l_st, acc_st, *, seq):
    i = pl.program_id(2)
    tq = DIL_TILE
    band = DIL_BAND
    C = HEAD_DIM
    assert LANES == 2 * C
    kwin[0:tq, :] = kp_ref[0, 0]
    kwin[tq:2 * tq, :] = kc_ref[0, 0]
    kwin[2 * tq:3 * tq, :] = kn_ref[0, 0]
    vwin[0:tq, :] = vp_ref[0, 0]
    vwin[tq:2 * tq, :] = vc_ref[0, 0]
    vwin[2 * tq:3 * tq, :] = vn_ref[0, 0]

    def rows(start, size, stride):
        return pl.ds(start, size) if stride == 1 else pl.ds(start, size, stride=stride)

    for pi, (window, dil) in enumerate(DIL_PATTERNS):
        assert window // (2 * dil) == band and band * dil <= tq
        qb = min(2 * band, tq // dil)
        nk = qb + 2 * band
        qa = lax.broadcasted_iota(jnp.int32, (qb, nk), 0)
        kj = lax.broadcasted_iota(jnp.int32, (qb, nk), 1)
        band_bias = jnp.where((kj >= qa) & (kj <= qa + 2 * band), 0.0, NEG)
        kcol = lax.broadcasted_iota(jnp.int32, (1, nk), 1)
        vlane = lax.broadcasted_iota(jnp.int32, (nk, LANES), 1)
        qlane = lax.broadcasted_iota(jnp.int32, (qb, LANES), 1)
        one = jnp.ones((nk, LANES), BF16)
        zero = jnp.zeros((nk, LANES), BF16)
        units = [(c, u) for c in range(dil) for u in range(tq // (dil * qb))]
        group = 4
        first = qlane < C
        for g0 in range(0, len(units), group):
            scores, values = [], []
            for c, u in units[g0:g0 + group]:
                q0 = c + dil * u * qb
                k0 = tq + q0 - dil * band
                pos = i * tq + (q0 - dil * band) + dil * kcol
                bias = band_bias + jnp.where((pos >= 0) & (pos < seq), 0.0, NEG)
                qr = q_ref[0, 0, rows(q0, qb, dil), :].astype(BF16)
                kr = kwin[rows(k0, nk, dil), :].astype(BF16)
                values.append(vwin[rows(k0, nk, dil), :].astype(BF16))
                if nk % LANES == 0:
                    k_bd = jnp.concatenate([jnp.where(vlane < C, kr, zero), jnp.where(vlane >= C, kr, zero)], axis=0)
                    s2 = _dot_nt(qr, k_bd)
                    scores.append(s2[:, 0:nk] + bias)
                    scores.append(s2[:, nk:2 * nk] + bias)
                else:
                    for hh in range(LANES // C):
                        sl = slice(hh * C, (hh + 1) * C)
                        scores.append(_dot_nt(qr[:, sl], kr[:, sl]) + bias)
            s = jnp.concatenate(scores, axis=0)
            m = jnp.max(s, axis=-1, keepdims=True)
            pb = jnp.exp(s - m).astype(BF16)
            for g, (c, u) in enumerate(units[g0:g0 + group]):
                b0 = slice(2 * g * qb, (2 * g + 1) * qb)
                b1 = slice((2 * g + 1) * qb, (2 * g + 2) * qb)
                pvl0 = _dot(pb[b0], jnp.where(vlane < C, values[g], one))
                pvl1 = _dot(pb[b1], jnp.where(vlane >= C, values[g], one))
                sel = rows(c + dil * u * qb, qb, dil)
                m_st[pi, sel, :] = jnp.where(first, m[b0], m[b1])
                l_st[pi, sel, :] = pltpu.roll(jnp.where(first, pvl1, pvl0), C, 1)
                acc_st[pi, sel, :] = jnp.where(first, pvl0, pvl1)
    m_all = m_st[0]
    for pi in range(1, len(DIL_PATTERNS)):
        m_all = jnp.maximum(m_all, m_st[pi])
    num = jnp.zeros((tq, LANES), F32)
    den = jnp.zeros((tq, LANES), F32)
    for pi in range(len(DIL_PATTERNS)):
        w = jnp.exp(m_st[pi] - m_all)
        num = num + w * acc_st[pi]
        den = den + w * l_st[pi]
    o_ref[0, 0] = num / den


def _dilated_attention(q, k, v):
    _, B, S, _ = q.shape
    t = DIL_TILE
    nq = S // t
    cur = pl.BlockSpec((1, 1, t, LANES), lambda b, p, i: (p, b, i, 0))
    prev = pl.BlockSpec((1, 1, t, LANES), lambda b, p, i: (p, b, jnp.maximum(i - 1, 0), 0))
    nxt = pl.BlockSpec((1, 1, t, LANES), lambda b, p, i: (p, b, jnp.minimum(i + 1, nq - 1), 0))
    return pl.pallas_call(
        functools.partial(_dil_kernel, seq=S), grid=(B, 2, nq),
        in_specs=[cur, prev, cur, nxt, prev, cur, nxt],
        out_specs=cur, out_shape=jax.ShapeDtypeStruct(q.shape, F32),
        scratch_shapes=[pltpu.VMEM((3 * t, LANES), F32), pltpu.VMEM((3 * t, LANES), F32),
                        pltpu.VMEM((3, t, LANES), F32), pltpu.VMEM((3, t, LANES), F32), pltpu.VMEM((3, t, LANES), F32)],
        compiler_params=_params(("parallel", "parallel", "parallel")), name="dilated_attn",
    )(q, k, k, k, v, v, v)


def _dot2(a, b):
    a_hi, a_lo = _split2(a)
    b_hi, b_lo = _split2(b)
    return _dot(a_hi, b_hi) + _dot(a_hi, b_lo) + _dot(a_lo, b_hi)


def _exact_rhs_dot(x, m):
    hi, mid, lo = _split3(x)
    return _dot(hi, m) + _dot(mid, m) + _dot(lo, m)


def _ssm_ops_kernel(lamr_row, lami_row, ldt_row, lamr_col, lami_col, ldt_col, brt_ref, bit_ref, crt_ref, cit_ref,
                    t_ref, p_ref, q_ref, al_ref):
    L, N, P = SSM_CHUNK, SSM_STATE, SSM_GROUP
    K = L * P
    lane_j = lax.broadcasted_iota(jnp.int32, (L, K), 1) // P
    row_j = lax.broadcasted_iota(jnp.int32, (L, K), 0)
    sel_r = lax.broadcasted_iota(jnp.int32, (K, L), 0) // P
    sel_j = lax.broadcasted_iota(jnp.int32, (K, L), 1)
    jcol = lax.broadcasted_iota(jnp.int32, (N, L), 1).astype(F32)
    jrow = lax.broadcasted_iota(jnp.int32, (L, N), 0).astype(F32)
    kcat, p_parts, q_parts, al_parts = [], [], [], []
    for d in range(2):
        expand = jnp.where(row_j == (lane_j if d == 0 else L - 1 - lane_j), 1.0, 0.0).astype(BF16)
        select = jnp.where(sel_j == (L - 1 - sel_r if d == 0 else sel_r), 1.0, 0.0).astype(BF16)
        lr_c, li_c, dt_c = lamr_col[d, 0], lami_col[d, 0], jnp.exp(ldt_col[d, 0])
        pmag = jnp.exp(jcol * (lr_c * dt_c))
        pr0 = _exact_rhs_dot(pmag * jnp.cos(jcol * (li_c * dt_c)), expand)
        pi0 = _exact_rhs_dot(pmag * jnp.sin(jcol * (li_c * dt_c)), expand)
        mag_c = jnp.exp(lr_c * dt_c)
        ar_c, ai_c = mag_c * jnp.cos(li_c * dt_c), mag_c * jnp.sin(li_c * dt_c)
        pr1, pi1 = ar_c * pr0 - ai_c * pi0, ar_c * pi0 + ai_c * pr0
        cr = jnp.concatenate([crt_ref[d, 0]] * (K // LANES), axis=1)
        ci = jnp.concatenate([cit_ref[d, 0]] * (K // LANES), axis=1)
        lr_r, li_r, dt_r = lamr_row[d, 0], lami_row[d, 0], jnp.exp(ldt_row[d, 0])
        mag_r = jnp.exp(lr_r * dt_r)
        ar_r, ai_r = mag_r * jnp.cos(li_r * dt_r), mag_r * jnp.sin(li_r * dt_r)
        den = lr_r * lr_r + li_r * li_r
        fr = ((ar_r - 1.0) * lr_r + ai_r * li_r) / den
        fi = (ai_r * lr_r - (ar_r - 1.0) * li_r) / den
        bbr = fr * brt_ref[d, 0] - fi * bit_ref[d, 0]
        bbi = fr * bit_ref[d, 0] + fi * brt_ref[d, 0]
        kcat.append(_dot2(bbr, pr0 * cr - pi0 * ci) - _dot2(bbi, pr0 * ci + pi0 * cr))
        q_parts.append((pr1 * cr - pi1 * ci, -(pi1 * cr + pr1 * ci)))
        rmag = jnp.exp(jrow * (lr_r * dt_r))
        prr = rmag * jnp.cos(jrow * (li_r * dt_r))
        pir = rmag * jnp.sin(jrow * (li_r * dt_r))
        prs, pis = _exact_lhs_dot(select, prr), _exact_lhs_dot(select, pir)
        bbr_t = jnp.concatenate([bbr] * L, axis=0)
        bbi_t = jnp.concatenate([bbi] * L, axis=0)
        p_parts.append((prs * bbr_t - pis * bbi_t, prs * bbi_t + pis * bbr_t))
        al_parts.append((ar_r * prr[L - 1:L] - ai_r * pir[L - 1:L], ar_r * pir[L - 1:L] + ai_r * prr[L - 1:L]))
    zeros = jnp.zeros((P, K), F32)
    ksum = (jnp.concatenate([zeros, kcat[0]], axis=1)
            + pltpu.roll(jnp.concatenate([kcat[1], zeros], axis=1), P, 1))
    per_tile = LANES // P
    for b in range(per_tile):
        rolled = ksum if b == 0 else pltpu.roll(ksum, P * b, 1)
        for a in range(L // per_tile):
            r = per_tile * a + b
            t_ref[0, P * r:P * (r + 1), :] = rolled[:, K - LANES * a:2 * K - LANES * a].astype(BF16)
    p_ref[0] = jnp.concatenate([p_parts[0][0], p_parts[1][0], p_parts[0][1], p_parts[1][1]], axis=1).astype(BF16)
    q_ref[0] = jnp.concatenate([q_parts[0][0], q_parts[1][0], q_parts[0][1], q_parts[1][1]], axis=0).astype(BF16)
    al_ref[0] = jnp.concatenate([jnp.concatenate([al_parts[0][0], al_parts[1][0]], axis=1),
                                 jnp.concatenate([al_parts[0][1], al_parts[1][1]], axis=1)], axis=0)


def _ssm_operators(lam_re, lam_im, log_dt, b_re, b_im, c_re, c_im):
    L, G, N, P = SSM_CHUNK, SSM_GROUPS, SSM_STATE, SSM_GROUP
    K = L * P
    rowf = lambda v: v.reshape(2, G, 1, N)
    colf = lambda v: v.reshape(2, G, N, 1)
    ldt = jnp.broadcast_to(log_dt[:, :, None], (2, G, N))
    tr = lambda v: jnp.swapaxes(v, 2, 3)
    ct = lambda v: jnp.tile(tr(v), (1, 1, 1, LANES // P))
    spec = lambda shape: pl.BlockSpec((2, 1) + shape, lambda g: (0, g, 0, 0))
    t_sum, p_cat, q_cat, al = pl.pallas_call(
        _ssm_ops_kernel, grid=(G,),
        in_specs=[spec((1, N))] * 3 + [spec((N, 1))] * 3 + [spec((P, N))] * 2 + [spec((N, LANES))] * 2,
        out_specs=[pl.BlockSpec((1, K, K), lambda g: (g, 0, 0)), pl.BlockSpec((1, K, 4 * N), lambda g: (g, 0, 0)),
                   pl.BlockSpec((1, 4 * N, K), lambda g: (g, 0, 0)), pl.BlockSpec((1, 2, 2 * N), lambda g: (g, 0, 0))],
        out_shape=[jax.ShapeDtypeStruct((G, K, K), BF16), jax.ShapeDtypeStruct((G, K, 4 * N), BF16),
                   jax.ShapeDtypeStruct((G, 4 * N, K), BF16), jax.ShapeDtypeStruct((G, 2, 2 * N), F32)],
        compiler_params=_params(("parallel",)), name="ssm_ops",
    )(rowf(lam_re), rowf(lam_im), rowf(ldt), colf(lam_re), colf(lam_im), colf(ldt),
      tr(b_re), tr(b_im), ct(c_re), ct(c_im))
    return t_sum, p_cat, q_cat, al[:, 0, :].reshape(1, G * 2 * N), al[:, 1, :].reshape(1, G * 2 * N)


def _ssm_state_in_kernel(u_ref, p_ref, xre_ref, xim_ref):
    x = _dot(u_ref[0].astype(BF16), p_ref[0])
    xre_ref[...] = x[:, 0:LANES]
    xim_ref[...] = x[:, LANES:2 * LANES]


def _ssm_state_in(u_t, p_cat):
    G, M, K = u_t.shape
    tm = min(M, 256)
    return pl.pallas_call(
        _ssm_state_in_kernel, grid=(G, M // tm),
        in_specs=[pl.BlockSpec((1, tm, K), lambda g, m: (g, m, 0)),
                  pl.BlockSpec((1, K, 2 * LANES), lambda g, m: (g, 0, 0))],
        out_specs=[pl.BlockSpec((tm, LANES), lambda g, m: (m, g))] * 2,
        out_shape=[jax.ShapeDtypeStruct((M, G * LANES), F32)] * 2,
        compiler_params=_params(("parallel", "parallel")), name="ssm_state_in",
    )(u_t, p_cat)


def _ssm_scan_kernel(xre_ref, xim_ref, are_ref, aim_ref, hfre_ref, hfim_ref, hbre_ref, hbim_ref):
    nc = xre_ref.shape[1]
    width = xre_ref.shape[2]
    ar = are_ref[...]
    ai = aim_ref[...]
    lane = lax.broadcasted_iota(jnp.int32, (1, width), 1)
    is_fwd = (lane % LANES) < SSM_STATE

    def step(i, carry):
        hr, hi = carry
        ib = nc - 1 - i
        hfre_ref[0, pl.ds(i, 1), :] = hr
        hfim_ref[0, pl.ds(i, 1), :] = hi
        hbre_ref[0, pl.ds(ib, 1), :] = hr
        hbim_ref[0, pl.ds(ib, 1), :] = hi
        xr = jnp.where(is_fwd, xre_ref[0, pl.ds(i, 1), :], xre_ref[0, pl.ds(ib, 1), :])
        xi = jnp.where(is_fwd, xim_ref[0, pl.ds(i, 1), :], xim_ref[0, pl.ds(ib, 1), :])
        return ar * hr - ai * hi + xr, ar * hi + ai * hr + xi

    zero = jnp.zeros((1, width), F32)
    lax.fori_loop(0, nc, step, (zero, zero))


def _ssm_scan(xre, xim, al_re, al_im):
    B, nc, width = xre.shape
    blk = pl.BlockSpec((1, nc, width), lambda b: (b, 0, 0))
    vec = pl.BlockSpec((1, width), lambda b: (0, 0))
    return pl.pallas_call(
        _ssm_scan_kernel, grid=(B,),
        in_specs=[blk, blk, vec, vec], out_specs=[blk] * 4,
        out_shape=[jax.ShapeDtypeStruct((B, nc, width), F32)] * 4,
        compiler_params=_params(("parallel",)), name="ssm_scan",
    )(xre, xim, al_re, al_im)


def _ssm_out_kernel(u_ref, t_ref, hfre_ref, hfim_ref, hbre_ref, hbim_ref, q_ref, y_ref):
    tm = u_ref.shape[1]
    lane = lax.broadcasted_iota(jnp.int32, (tm, LANES), 1)
    is_fwd = lane < SSM_STATE
    h_re = jnp.where(is_fwd, hfre_ref[...], hbre_ref[...])
    h_im = jnp.where(is_fwd, hfim_ref[...], hbim_ref[...])
    h_cat = jnp.concatenate([h_re, h_im], axis=1).astype(BF16)
    y_ref[0] = _dot(u_ref[0].astype(BF16), t_ref[0]) + _dot(h_cat, q_ref[0])


def _ssm_out(u_t, t_sum, hs, q_cat):
    G, M, K = u_t.shape
    tm = min(M, 256)
    hblk = pl.BlockSpec((tm, LANES), lambda g, m: (m, g))
    return pl.pallas_call(
        _ssm_out_kernel, grid=(G, M // tm),
        in_specs=[pl.BlockSpec((1, tm, K), lambda g, m: (g, m, 0)),
                  pl.BlockSpec((1, K, K), lambda g, m: (g, 0, 0)),
                  hblk, hblk, hblk, hblk,
                  pl.BlockSpec((1, 2 * LANES, K), lambda g, m: (g, 0, 0))],
        out_specs=pl.BlockSpec((1, tm, K), lambda g, m: (g, m, 0)),
        out_shape=jax.ShapeDtypeStruct((G, M, K), F32),
        compiler_params=_params(("parallel", "parallel")), name="ssm_out",
    )(u_t, t_sum, *hs, q_cat)


def _s5_scan(u_t, ops, batch):
    t_sum, p_cat, q_cat, al_re, al_im = ops
    G, M, _ = u_t.shape
    nc = M // batch
    xre, xim = _ssm_state_in(u_t, p_cat)
    width = G * LANES
    hs = _ssm_scan(xre.reshape(batch, nc, width), xim.reshape(batch, nc, width), al_re, al_im)
    return _ssm_out(u_t, t_sum, tuple(h.reshape(M, width) for h in hs), q_cat)


def _na_bias_kernel(rpb_ref, out_ref):
    h = pl.program_id(0)
    n_dr, n_dc = 2 * NA_WIN_R - 1, 2 * NA_WIN_C - 1
    qc = lax.broadcasted_iota(jnp.int32, (GRID_W, GRID_W), 0)
    kc = lax.broadcasted_iota(jnp.int32, (GRID_W, GRID_W), 1)
    dc = jnp.clip(kc - qc + (NA_WIN_C - 1), 0, n_dc - 1)
    cstart = jnp.clip(qc - NA_WIN_C // 2, 0, GRID_W - NA_WIN_C)
    col_ok = (kc >= cstart) & (kc < cstart + NA_WIN_C)
    base = []
    for dr in range(n_dr):
        acc = jnp.zeros((GRID_W, GRID_W), F32)
        for d in range(n_dc):
            acc = jnp.where(dc == d, rpb_ref[(h * n_dr + dr) * n_dc + d], acc)
        base.append(jnp.where(col_ok, acc, NEG))
    for off in range(NA_WIN_R):
        out_ref[0, off] = jnp.concatenate([base[w - off + NA_WIN_R - 1] for w in range(NA_WIN_R)], axis=1)


def _na_bias_table(rpb):
    return pl.pallas_call(
        _na_bias_kernel, grid=(HEADS,),
        in_specs=[pl.BlockSpec(memory_space=pltpu.SMEM)],
        out_specs=pl.BlockSpec((1, NA_WIN_R, GRID_W, NA_WIN_R * GRID_W), lambda h: (h, 0, 0, 0)),
        out_shape=jax.ShapeDtypeStruct((HEADS, NA_WIN_R, GRID_W, NA_WIN_R * GRID_W), F32),
        compiler_params=_params(("parallel",)), name="na_bias",
    )(rpb.astype(F32).reshape(-1))


def _na_kernel(q_ref, kp_ref, kc_ref, kn_ref, vp_ref, vc_ref, vn_ref, bias_ref, o_ref, kbuf, vbuf, *, rows):
    rb = pl.program_id(1)
    Wg = GRID_W
    C = HEAD_DIM
    halo = NA_HALO_ROWS * Wg
    cur = NA_ROWS_PER_STEP * Wg
    win = NA_WIN_R * Wg
    kbuf[0:halo, :] = kp_ref[0]
    kbuf[halo:halo + cur, :] = kc_ref[0]
    kbuf[halo + cur:2 * halo + cur, :] = kn_ref[0]
    vbuf[0:halo, :] = vp_ref[0]
    vbuf[halo:halo + cur, :] = vc_ref[0]
    vbuf[halo + cur:2 * halo + cur, :] = vn_ref[0]
    r0 = rb * NA_ROWS_PER_STEP
    pairs = HEADS // 2
    assert LANES == 2 * C
    klane = lax.broadcasted_iota(jnp.int32, (win, LANES), 1)
    olane = lax.broadcasted_iota(jnp.int32, (Wg, LANES), 1)
    zero = jnp.zeros((win, LANES), BF16)
    one = jnp.ones((win, LANES), BF16)
    group = 4
    for j0 in range(0, NA_ROWS_PER_STEP, group):
        scores, values = [], []
        for j in range(j0, j0 + group):
            r = r0 + j
            rs = jnp.clip(r - NA_WIN_R // 2, 0, rows - NA_WIN_R)
            off = r - rs
            start = pl.multiple_of((rs - r0 + NA_HALO_ROWS) * Wg, Wg)
            kw = kbuf[pl.ds(start, win), :]
            values.append(vbuf[pl.ds(start, win), :])
            for pr in range(pairs):
                tile = slice(pr * LANES, (pr + 1) * LANES)
                k_bd = jnp.concatenate([jnp.where(klane < C, kw[:, tile], zero),
                                        jnp.where(klane >= C, kw[:, tile], zero)], axis=0)
                s2 = _dot_nt(q_ref[0, j * Wg:(j + 1) * Wg, tile], k_bd)
                scores.append(s2[:, 0:win] + bias_ref[2 * pr, off])
                scores.append(s2[:, win:2 * win] + bias_ref[2 * pr + 1, off])
        s = jnp.concatenate(scores, axis=0)
        pb = jnp.exp(s - jnp.max(s, axis=-1, keepdims=True)).astype(BF16)
        for g in range(group):
            outs = []
            for pr in range(pairs):
                tile = slice(pr * LANES, (pr + 1) * LANES)
                b0 = slice((g * HEADS + 2 * pr) * Wg, (g * HEADS + 2 * pr + 1) * Wg)
                b1 = slice((g * HEADS + 2 * pr + 1) * Wg, (g * HEADS + 2 * pr + 2) * Wg)
                pvl0 = _dot(pb[b0], jnp.where(klane < C, values[g][:, tile], one))
                pvl1 = _dot(pb[b1], jnp.where(klane >= C, values[g][:, tile], one))
                first = olane < C
                outs.append(jnp.where(first, pvl0, pvl1) / pltpu.roll(jnp.where(first, pvl1, pvl0), C, 1))
            o_ref[0, (j0 + g) * Wg:(j0 + g + 1) * Wg, :] = jnp.concatenate(outs, axis=1)


def _neighbourhood_attention(q, k, v, bias):
    B, S, W = q.shape
    rows = S // GRID_W
    cur = NA_ROWS_PER_STEP * GRID_W
    halo = NA_HALO_ROWS * GRID_W
    nrb = S // cur
    per = cur // halo
    nh = S // halo
    blk = pl.BlockSpec((1, cur, W), lambda b, i: (b, i, 0))
    prev = pl.BlockSpec((1, halo, W), lambda b, i: (b, jnp.maximum(i * per - 1, 0), 0))
    nxt = pl.BlockSpec((1, halo, W), lambda b, i: (b, jnp.minimum((i + 1) * per, nh - 1), 0))
    bspec = pl.BlockSpec(bias.shape, lambda b, i: (0, 0, 0, 0))
    return pl.pallas_call(
        functools.partial(_na_kernel, rows=rows), grid=(B, nrb),
        in_specs=[blk, prev, blk, nxt, prev, blk, nxt, bspec],
        out_specs=blk, out_shape=jax.ShapeDtypeStruct((B, S, W), F32),
        scratch_shapes=[pltpu.VMEM((cur + 2 * halo, W), BF16), pltpu.VMEM((cur + 2 * halo, W), BF16)],
        compiler_params=_params(("parallel", "parallel")), name="neighbourhood_attn",
    )(q, k, k, k, v, v, v, bias)


def _post_kernel(x_ref, of_ref, ob_ref, gate_ref, yb_ref, yt_ref, u_ref, yd_ref,
                 dnw_ref, bd_ref, dskip_ref, gluw_ref, glub_ref, wout_ref, ln2_ref, w1_ref, w2_ref, fnw_ref,
                 out_ref, ybuf, *, final):
    x = x_ref[...]
    o = of_ref[...] + ob_ref[...]
    ms = _head_sums(o * o, bd_ref[...]) * (1.0 / HEAD_DIM)
    g = gate_ref[...]
    y_a = o * lax.rsqrt(ms + EPS) * dnw_ref[...] * (g * _sigmoid(g))
    y = _chunk_unflatten(yt_ref, ybuf) + dskip_ref[...] * u_ref[...]
    z = 0.5 * y * (1.0 + jnp.tanh(math.sqrt(2.0 / math.pi) * (y + 0.044715 * (y * y * y))))
    y_c = z * _sigmoid(_dot(z.astype(BF16), gluw_ref[...]) + glub_ref[...])
    mix = jnp.concatenate([y_a, yb_ref[0], yb_ref[1], y_c, yd_ref[...]], axis=1).astype(BF16)
    x1 = x + _dot(mix, wout_ref[...])
    hb = _rms(x1, ln2_ref[...]).astype(BF16)
    ffc = D_MODEL
    acc = x1
    for c in range(D_FF // ffc):
        a = jnp.maximum(_dot(hb, w1_ref[:, c * ffc:(c + 1) * ffc]), 0.0)
        acc = acc + _dot((a * a).astype(BF16), w2_ref[c * ffc:(c + 1) * ffc, :])
    if final:
        acc = _rms(acc, fnw_ref[...])
    out_ref[...] = acc


def _post(x, o_f, o_b, gate, y_b, y_t, u, y_d, wts, final):
    T = x.shape[0]
    tm = min(TOKEN_TILE, T)
    W = WIDTH
    row = lambda n: pl.BlockSpec((tm, n), lambda i: (i, 0))
    const = lambda a: pl.BlockSpec(a.shape, lambda i: (0, 0), pipeline_mode=pl.Buffered(1))
    return pl.pallas_call(
        functools.partial(_post_kernel, final=final), grid=(T // tm,),
        in_specs=([row(D_MODEL)] + [row(W)] * 3 + [pl.BlockSpec((2, tm, LANES), lambda i: (0, i, 0)),
                                                   pl.BlockSpec((SSM_GROUPS, tm // SSM_CHUNK, y_t.shape[2]),
                                                                lambda i: (0, i, 0))] + [row(W)] * 2
                  + [const(a) for a in wts]),
        out_specs=row(D_MODEL), out_shape=jax.ShapeDtypeStruct((T, D_MODEL), F32),
        scratch_shapes=[pltpu.VMEM((W // LANES, tm, LANES), F32)],
        compiler_params=_params(("parallel",)), name="post",
    )(x, o_f, o_b, gate, y_b, y_t, u, y_d, *wts)


def _rope_tables(seq):
    inv_freq = ROPE_THETA ** (-jnp.arange(0, HEAD_DIM, 2, dtype=F32) / HEAD_DIM)
    ang = jnp.arange(seq, dtype=F32)[:, None] * inv_freq[None, :]
    cos, sin = jnp.cos(ang), jnp.sin(ang)
    cos_t = jnp.tile(jnp.concatenate([cos, cos], axis=1), (1, HEADS))
    sin_t = jnp.tile(jnp.concatenate([-sin, sin], axis=1), (1, HEADS))
    return cos_t, sin_t


def _layer_weights(i, p):
    W = WIDTH
    w = p['w_in'][i]
    o_gate, o_a, o_b, o_dil, o_ssm, o_na = 3 * W, 4 * W, 4 * W + 2 * HEADS, 4 * W + 4 * HEADS, 7 * W + 4 * HEADS, 8 * W + 4 * HEADS
    pad = jnp.zeros((D_MODEL, LANES - 4 * HEADS), F32)
    w_in = jnp.concatenate([w[:, 0:o_a], w[:, o_dil:o_na + 3 * W], w[:, o_a:o_dil], pad], axis=1).astype(BF16)
    assert w_in.shape[1] == IN_COLS_PADDED
    lane_pad = lambda v: jnp.concatenate([v.reshape(1, -1).astype(F32), jnp.zeros((1, LANES - v.size), F32)], axis=1)
    idx = np.arange(W)
    bd = jnp.asarray((idx[:, None] // HEAD_DIM) == (idx[None, :] // HEAD_DIM), BF16)
    row = lambda v: v.reshape(1, -1).astype(F32)
    return dict(
        ln1=row(p['ln1_w'][i]), w_in=w_in,
        conv_w=p['dn_conv_w'][i].astype(F32), alog=lane_pad(p['dn_a_log'][i]), dtb=lane_pad(p['dn_dt_bias'][i]),
        bd=bd,
        ssm=_ssm_operators(*(p[n][i].astype(F32) for n in
                             ('ssm_lam_re', 'ssm_lam_im', 'ssm_log_dt', 'ssm_b_re', 'ssm_b_im', 'ssm_c_re', 'ssm_c_im'))),
        na_bias=_na_bias_table(p['na_rpb'][i]),
        post=(row(jnp.tile(p['dn_norm_w'][i], HEADS)), bd, row(p['ssm_d'][i]), p['ssm_glu_w'][i].astype(BF16),
              row(p['ssm_glu_b'][i]), p['w_out'][i].astype(BF16), row(p['ln2_w'][i]),
              p['w_ff1'][i].astype(BF16), p['w_ff2'][i].astype(BF16), row(p['final_norm_w'])),
    )


def _trunk(x, layers):
    B, S, D = x.shape
    T = B * S
    cos_t, sin_t = _rope_tables(S)
    xt = x.reshape(T, D)
    seq3 = lambda a: a.reshape(B, S, a.shape[-1])
    flat = lambda a: a.reshape(T, a.shape[-1])
    for i, lw in enumerate(layers):
        q, k, v, gb, z_gate, dq, dk, dv, z_ssm, u_t, nq, nk, nv = _in_proj(
            xt, lw['ln1'], lw['w_in'], cos_t, sin_t, lw['conv_w'], lw['alog'], lw['dtb'], lw['bd'], S)
        o_f, o_b = _dn_scan(seq3(q), seq3(k), seq3(v), seq3(gb))
        pairs = lambda a: a.reshape(2, B, S, LANES)
        y_b = _dilated_attention(pairs(dq), pairs(dk), pairs(dv)).reshape(2, T, LANES)
        y_t = _s5_scan(u_t, lw['ssm'], B)
        y_d = _neighbourhood_attention(seq3(nq), seq3(nk), seq3(nv), lw['na_bias'])
        xt = _post(xt, flat(o_f), flat(o_b), z_gate, y_b, y_t, z_ssm, flat(y_d),
                   lw['post'], final=(i == len(layers) - 1))
    return xt.reshape(B, S, D)


def kernel(x_prompt, x_sample, ln1_w, w_in, dn_conv_w, dn_a_log, dn_dt_bias, dn_norm_w, ssm_lam_re, ssm_lam_im,
           ssm_log_dt, ssm_b_re, ssm_b_im, ssm_c_re, ssm_c_im, ssm_d, ssm_glu_w, ssm_glu_b, na_rpb, w_out,
           ln2_w, w_ff1, w_ff2, final_norm_w):
    p = dict(ln1_w=ln1_w, w_in=w_in, dn_conv_w=dn_conv_w, dn_a_log=dn_a_log, dn_dt_bias=dn_dt_bias,
             dn_norm_w=dn_norm_w, ssm_lam_re=ssm_lam_re, ssm_lam_im=ssm_lam_im, ssm_log_dt=ssm_log_dt,
             ssm_b_re=ssm_b_re, ssm_b_im=ssm_b_im, ssm_c_re=ssm_c_re, ssm_c_im=ssm_c_im, ssm_d=ssm_d,
             ssm_glu_w=ssm_glu_w, ssm_glu_b=ssm_glu_b, na_rpb=na_rpb, w_out=w_out, ln2_w=ln2_w,
             w_ff1=w_ff1, w_ff2=w_ff2, final_norm_w=final_norm_w)
    layers = [_layer_weights(i, p) for i in range(DEPTH)]
    return (_trunk(x_prompt, layers), _trunk(x_sample, layers))
```
